```python
import jax, jax.numpy as jnp
from jax import lax
import numpy as np

D_MODEL = 1024
BATCH = 16
SEQ = 2048
DEPTH = 1
DEC_BATCH = 128
DEC_SEQ = 1
PAST_LEN = 8192
PAGE_SIZE = 128

N_HEADS = 8
HEAD_DIM = 64
ATTN_W = N_HEADS * HEAD_DIM
MOBA_BLOCK = 256
MOBA_TOPK = 3
Q_BLOCK = 128
SGU_GROUPS = 8
SGU_W = 512
SGU_GROUP_W = SGU_W // SGU_GROUPS
SGU_CHUNK = 128
D_FF = 4 * D_MODEL
ROPE_THETA = 10000.0
EPS = 1e-6
NEG = -1e30
IN_W = 3 * ATTN_W + 2 * SGU_W + 2 * D_MODEL

kernel_name = "moba_gmlp_gated_hybrid_step"

F32 = jnp.float32


def rmsnorm(x, g):
    xf = x.astype(F32)
    y = xf * lax.rsqrt(jnp.mean(xf * xf, axis=-1, keepdims=True) + EPS)
    return (y * g.astype(F32)).astype(x.dtype)


def rope(x, pos):
    half = HEAD_DIM // 2
    inv = ROPE_THETA ** (-2.0 * jnp.arange(half, dtype=F32) / HEAD_DIM)
    ang = pos.astype(F32)[:, None] * inv[None, :]
    c = jnp.cos(ang)[None, :, None, :]
    s = jnp.sin(ang)[None, :, None, :]
    xf = x.astype(F32)
    x1, x2 = xf[..., :half], xf[..., half:]
    return jnp.concatenate([x1 * c - x2 * s, x2 * c + x1 * s], axis=-1).astype(x.dtype)


def moba_sequence(q, k, v, q_pos0):
    T = k.shape[0]
    nb = max(-(-T // MOBA_BLOCK), MOBA_TOPK)
    t_pad = nb * MOBA_BLOCK
    kp = jnp.pad(k, ((0, t_pad - T), (0, 0), (0, 0)))
    vp = jnp.pad(v, ((0, t_pad - T), (0, 0), (0, 0)))
    kb = kp.reshape(nb, MOBA_BLOCK, N_HEADS, HEAD_DIM).transpose(2, 0, 1, 3)
    vb = vp.reshape(nb, MOBA_BLOCK, N_HEADS, HEAD_DIM).transpose(2, 0, 1, 3)
    k_mean = jnp.mean(kb.astype(F32), axis=2)
    Lq = q.shape[0]
    qb = min(Q_BLOCK, Lq)
    n_qb = -(-Lq // qb)
    qp = jnp.pad(q, ((0, n_qb * qb - Lq), (0, 0), (0, 0))).reshape(n_qb, qb, N_HEADS, HEAD_DIM)
    scale = HEAD_DIM ** -0.5

    def one_block(args):
        qblk, i = args
        pos = q_pos0 + i * qb + jnp.arange(qb)
        own = jnp.minimum(pos // MOBA_BLOCK, nb - 1)
        gate = jnp.einsum('qhd,hnd->hqn', qblk.astype(F32), k_mean)
        past = jnp.arange(nb)[None, None, :] < own[None, :, None]
        gate = jnp.where(past, gate, NEG)
        _, sel = lax.top_k(gate, MOBA_TOPK)
        sel_ok = sel < own[None, :, None]
        blocks = jnp.concatenate(
            [sel, jnp.broadcast_to(own[None, :, None], (N_HEADS, qb, 1))], axis=-1)
        kg = jax.vmap(lambda kh, ih: kh[ih])(kb, blocks)
        vg = jax.vmap(lambda vh, ih: vh[ih])(vb, blocks)
        s = jnp.einsum('qhd,hqnkd->hqnk', qblk, kg, preferred_element_type=F32) * scale
        own_ok = (own[:, None] * MOBA_BLOCK + jnp.arange(MOBA_BLOCK)[None, :]) <= pos[:, None]
        ok = jnp.concatenate([
            jnp.broadcast_to(sel_ok[..., None], (N_HEADS, qb, MOBA_TOPK, MOBA_BLOCK)),
            jnp.broadcast_to(own_ok[None, :, None, :], (N_HEADS, qb, 1, MOBA_BLOCK))], axis=2)
        s = jnp.where(ok, s, NEG)
        p = jax.nn.softmax(s.reshape(N_HEADS, qb, -1), axis=-1).reshape(s.shape)
        o = jnp.einsum('hqnk,hqnkd->qhd', p.astype(vg.dtype), vg, preferred_element_type=F32)
        return o.astype(q.dtype)

    out = lax.map(one_block, (qp, jnp.arange(n_qb)))
    return out.reshape(n_qb * qb, N_HEADS, HEAD_DIM)[:Lq]


def spatial_gating(z, w_s, b_s, g_v):
    B, L, _ = z.shape
    u, vv = z[..., :SGU_W], z[..., SGU_W:]
    vv = rmsnorm(vv, g_v)
    n_c = -(-L // SGU_CHUNK)
    vpad = jnp.pad(vv, ((0, 0), (0, n_c * SGU_CHUNK - L), (0, 0)))
    vpad = vpad.reshape(B, n_c, SGU_CHUNK, SGU_GROUPS, SGU_GROUP_W)
    ws = w_s * jnp.tril(jnp.ones((SGU_CHUNK, SGU_CHUNK), w_s.dtype))[None]
    s = jnp.einsum('gts,bcsgd->bctgd', ws, vpad) + b_s.T[None, None, :, :, None]
    s = s.reshape(B, n_c * SGU_CHUNK, SGU_W)[:, :L]
    return u * s, vv


def hybrid_layer(x, pos, attend, g_attn, w_in, w_pa, w_pb, w_o, g_v, w_s, b_s, g_ffn, w_up, w_down):
    B, L, _ = x.shape
    h = rmsnorm(x, g_attn)
    proj = h @ w_in
    q, k, v, z, ga, gb = jnp.split(
        proj, [ATTN_W, 2 * ATTN_W, 3 * ATTN_W, 3 * ATTN_W + 2 * SGU_W,
               3 * ATTN_W + 2 * SGU_W + D_MODEL], axis=-1)
    q = rope(q.reshape(B, L, N_HEADS, HEAD_DIM), pos)
    k = rope(k.reshape(B, L, N_HEADS, HEAD_DIM), pos)
    v = v.reshape(B, L, N_HEADS, HEAD_DIM)
    o_a = attend(q, k, v)
    o_b, sgu_v = spatial_gating(jax.nn.gelu(z), w_s, b_s, g_v)
    branch_a = o_a.reshape(B, L, ATTN_W) @ w_pa
    branch_b = o_b @ w_pb
    merged = jax.nn.sigmoid(ga) * branch_a + jax.nn.sigmoid(gb) * branch_b
    x = x + merged @ w_o
    h2 = rmsnorm(x, g_ffn)
    x = x + jnp.square(jax.nn.relu(h2 @ w_up)) @ w_down
    return x, k, v, sgu_v


def setup_inputs(seed: int = 0) -> dict:
    key = jax.random.key(seed)
    ks = jax.random.split(key, 20)
    n_pages = PAST_LEN // PAGE_SIZE
    n_phys = (DEC_BATCH * n_pages * 5) // 4
    nrm = jax.random.normal
    perm = jax.random.permutation(ks[4], n_phys)[:DEC_BATCH * n_pages]
    return {
        "x_prompt": nrm(ks[0], (BATCH, SEQ, D_MODEL), F32),
        "x_sample": nrm(ks[1], (DEC_BATCH, DEC_SEQ, D_MODEL), F32),
        "cache_k": nrm(ks[2], (DEPTH, n_phys, PAGE_SIZE, N_HEADS, HEAD_DIM), F32),
        "cache_v": nrm(ks[3], (DEPTH, n_phys, PAGE_SIZE, N_HEADS, HEAD_DIM), F32),
        "page_table": perm.reshape(DEC_BATCH, n_pages).astype(jnp.int32),
        "g_attn": 1.0 + 0.01 * nrm(ks[5], (DEPTH, D_MODEL), F32),
        "w_in": nrm(ks[6], (DEPTH, D_MODEL, IN_W), F32) * D_MODEL ** -0.5,
        "w_pa": nrm(ks[7], (DEPTH, ATTN_W, D_MODEL), F32) * ATTN_W ** -0.5,
        "w_pb": nrm(ks[8], (DEPTH, SGU_W, D_MODEL), F32) * SGU_W ** -0.5,
        "w_o": nrm(ks[9], (DEPTH, D_MODEL, D_MODEL), F32) * D_MODEL ** -0.5,
        "g_v": 1.0 + 0.01 * nrm(ks[10], (DEPTH, SGU_W), F32),
        "w_s": nrm(ks[11], (DEPTH, SGU_GROUPS, SGU_CHUNK, SGU_CHUNK), F32) * SGU_CHUNK ** -0.5,
        "b_s": 1.0 + 0.1 * nrm(ks[12], (DEPTH, SGU_GROUPS, SGU_CHUNK), F32),
        "g_ffn": 1.0 + 0.01 * nrm(ks[13], (DEPTH, D_MODEL), F32),
        "w_up": nrm(ks[14], (DEPTH, D_MODEL, D_FF), F32) * D_MODEL ** -0.5,
        "w_down": nrm(ks[15], (DEPTH, D_FF, D_MODEL), F32) * D_FF ** -0.5,
        "g_final": 1.0 + 0.01 * nrm(ks[16], (D_MODEL,), F32),
    }


def reference(x_prompt, x_sample, cache_k, cache_v, page_table, g_attn, w_in, w_pa, w_pb, w_o,
              g_v, w_s, b_s, g_ffn, w_up, w_down, g_final):
    pos_p = jnp.arange(SEQ, dtype=jnp.int32)
    pos_s = PAST_LEN + jnp.arange(DEC_SEQ, dtype=jnp.int32)
    last_chunk_start = ((SEQ - 1) // SGU_CHUNK) * SGU_CHUNK

    def attend_prompt(q, k, v):
        return lax.map(lambda a: moba_sequence(a[0], a[1], a[2], 0), (q, k, v))

    xp, xs = x_prompt, x_sample
    kp_l, vp_l, gp_l, ks_l, vs_l, gs_l = [], [], [], [], [], []
    for l in range(DEPTH):
        def attend_sample(q, k, v, l=l):
            def per_seq(a):
                qi, ki, vi, pt = a
                k_past = cache_k[l, pt].reshape(-1, N_HEADS, HEAD_DIM)
                v_past = cache_v[l, pt].reshape(-1, N_HEADS, HEAD_DIM)
                k_all = jnp.concatenate([k_past, ki.astype(k_past.dtype)], axis=0)
                v_all = jnp.concatenate([v_past, vi.astype(v_past.dtype)], axis=0)
                return moba_sequence(qi, k_all, v_all, PAST_LEN).astype(qi.dtype)
            return lax.map(per_seq, (q, k, v, page_table))

        params = (g_attn[l], w_in[l], w_pa[l], w_pb[l], w_o[l], g_v[l], w_s[l], b_s[l],
                  g_ffn[l], w_up[l], w_down[l])
        xp, kp, vp, gp = hybrid_layer(xp, pos_p, attend_prompt, *params)
        xs, ksn, vsn, gsn = hybrid_layer(xs, pos_s, attend_sample, *params)
        kp_l.append(kp); vp_l.append(vp); gp_l.append(gp[:, last_chunk_start:])
        ks_l.append(ksn); vs_l.append(vsn); gs_l.append(gsn)

    y_prompt = rmsnorm(xp, g_final)
    y_sample = rmsnorm(xs, g_final)
    k_prompt = jnp.stack(kp_l)
    v_prompt = jnp.stack(vp_l)
    sgu_v_prompt = jnp.stack(gp_l)
    k_sample = jnp.stack(ks_l)
    v_sample = jnp.stack(vs_l)
    sgu_v_sample = jnp.stack(gs_l)
    return (y_prompt, y_sample, k_prompt, v_prompt, sgu_v_prompt, k_sample, v_sample, sgu_v_sample)
```

```python
import functools

import jax
import jax.numpy as jnp
from jax import lax
from jax.experimental import pallas as pl
from jax.experimental.pallas import tpu as pltpu

N_HEADS = 8
HEAD_DIM = 64
ATTN_W = N_HEADS * HEAD_DIM
MOBA_BLOCK = 256
MOBA_TOPK = 3
SGU_GROUPS = 8
SGU_W = 512
SGU_GROUP_W = SGU_W // SGU_GROUPS
SGU_CHUNK = 128
ROPE_THETA = 10000.0
EPS = 1e-6
NEG = -1e30
SCALE = HEAD_DIM ** -0.5

LANES = 128
HEADS_PER_LANE_BLOCK = LANES // HEAD_DIM
VMEM_LIMIT = 56 * 1024 * 1024

F32 = jnp.float32
BF16 = jnp.bfloat16
NT_DIMS = (((1,), (1,)), ((), ()))


def _rmsnorm(x, g):
    return x * lax.rsqrt(jnp.mean(x * x, axis=-1, keepdims=True) + EPS) * g


def _const_spec(shape):
    return pl.BlockSpec(shape, lambda *_: (0,) * len(shape), pipeline_mode=pl.Buffered(1))


def _rope(t, cos, sin_signed):
    lane = lax.broadcasted_iota(jnp.int32, (t.shape[0], LANES), 1)
    first_half = (lane & (HEAD_DIM - 1)) < HEAD_DIM // 2
    outs = []
    for c in range(t.shape[1] // LANES):
        sl = slice(c * LANES, (c + 1) * LANES)
        tc = t[:, sl]
        partner = jnp.where(first_half, pltpu.roll(tc, LANES - HEAD_DIM // 2, 1),
                            pltpu.roll(tc, HEAD_DIM // 2, 1))
        outs.append(tc * cos[:, sl] + partner * sin_signed[:, sl])
    return jnp.concatenate(outs, axis=1)


def _inproj_kernel(x_ref, g_ref, w_ref, cos_ref, sin_ref, gv_ref,
                   q_ref, kf_ref, vf_ref, kb_ref, vb_ref, u_ref, vv_ref, sga_ref, sgb_ref, sguv_ref,
                   *maybe_kmean_ref):
    d_model = x_ref.shape[1]
    h = _rmsnorm(x_ref[...], g_ref[...]).astype(BF16)

    def proj(lo, width):
        return jnp.dot(h, w_ref[:, lo:lo + width], preferred_element_type=F32)

    cos = cos_ref[...]
    sin = sin_ref[...]
    q = _rope(proj(0, ATTN_W), cos, sin)
    q_ref[...] = (q * SCALE).astype(q_ref.dtype)
    k = _rope(proj(ATTN_W, ATTN_W), cos, sin)
    kf_ref[...] = k
    kb_ref[...] = k.astype(BF16)
    if maybe_kmean_ref:
        maybe_kmean_ref[0][0] = jnp.mean(k, axis=0, keepdims=True)
    v = proj(2 * ATTN_W, ATTN_W)
    vf_ref[...] = v
    vb_ref[...] = v.astype(BF16)
    base = 3 * ATTN_W
    u_ref[...] = jax.nn.gelu(proj(base, SGU_W)).astype(BF16)
    vv = _rmsnorm(jax.nn.gelu(proj(base + SGU_W, SGU_W)), gv_ref[...])
    vv_ref[...] = vv.astype(BF16)
    sguv_ref[...] = vv[vv.shape[0] - sguv_ref.shape[0]:, :]
    base += 2 * SGU_W
    sga_ref[...] = jax.nn.sigmoid(proj(base, d_model)).astype(BF16)
    sgb_ref[...] = jax.nn.sigmoid(proj(base + d_model, d_model)).astype(BF16)


def _inproj(x2d, g_attn, w_in_bf, cos_tab, sin_tab, g_v, *, tm, seq_tiles, q_dtype, emit_kmean):
    m, d = x2d.shape
    n_tiles = m // tm
    n_seq = n_tiles // seq_tiles
    keep = min(SGU_CHUNK, tm)
    row = lambda w: pl.BlockSpec((tm, w), lambda r: (r, 0))
    tab = pl.BlockSpec((tm, ATTN_W), lambda r: (r % seq_tiles, 0))
    out_shape = [
        jax.ShapeDtypeStruct((m, ATTN_W), q_dtype),
        jax.ShapeDtypeStruct((m, ATTN_W), F32),
        jax.ShapeDtypeStruct((m, ATTN_W), F32),
        jax.ShapeDtypeStruct((m, ATTN_W), BF16),
        jax.ShapeDtypeStruct((m, ATTN_W), BF16),
        jax.ShapeDtypeStruct((m, SGU_W), BF16),
        jax.ShapeDtypeStruct((m, SGU_W), BF16),
        jax.ShapeDtypeStruct((m, d), BF16),
        jax.ShapeDtypeStruct((m, d), BF16),
        jax.ShapeDtypeStruct((n_seq * keep, SGU_W), F32),
    ]
    out_specs = [row(ATTN_W)] * 5 + [row(SGU_W)] * 2 + [row(d)] * 2 + [
        pl.BlockSpec((keep, SGU_W), lambda r: (r // seq_tiles, 0))]
    if emit_kmean:
        out_shape.append(jax.ShapeDtypeStruct((n_tiles, 1, ATTN_W), F32))
        out_specs.append(pl.BlockSpec((1, 1, ATTN_W), lambda r: (r, 0, 0)))
    return pl.pallas_call(
        _inproj_kernel,
        grid=(n_tiles,),
        in_specs=[row(d), _const_spec((1, d)), _const_spec(w_in_bf.shape), tab, tab, _const_spec((1, SGU_W))],
        out_specs=out_specs,
        out_shape=out_shape,
        compiler_params=pltpu.CompilerParams(dimension_semantics=("arbitrary",), vmem_limit_bytes=VMEM_LIMIT),
        name="inproj",
    )(x2d, g_attn.reshape(1, d), w_in_bf, cos_tab, sin_tab, g_v.reshape(1, SGU_W))


def _moba_select(gate, own, n_blocks):
    col = lax.broadcasted_iota(jnp.int32, gate.shape, 1)
    gate = jnp.where(col < own, gate, NEG * SCALE)
    sel = []
    for j in range(n_blocks):
        gj = gate[:, j:j + 1]
        beats = jnp.where(gate > gj, 1.0, jnp.where((gate == gj) & (col < j), 1.0, 0.0))
        sel.append(jnp.sum(beats, axis=1, keepdims=True) < MOBA_TOPK)
    return sel


def _prompt_attn_kernel(q_ref, k_ref, v_ref, km_ref, o_ref, m_sc, l_sc, acc_sc, *, n_blocks):
    i = pl.program_id(2)
    tq = MOBA_BLOCK
    g_n = HEADS_PER_LANE_BLOCK
    q = q_ref[0]
    lane = lax.broadcasted_iota(jnp.int32, (tq, LANES), 1)
    qs = jnp.concatenate([jnp.where(lane // HEAD_DIM == g, q, jnp.zeros_like(q)) for g in range(g_n)], axis=0)

    km = km_ref[0]
    km_hi = km.astype(BF16)
    km_lo = (km - km_hi.astype(F32)).astype(BF16)
    gate = (lax.dot_general(qs, km_hi, NT_DIMS, preferred_element_type=F32)
            + lax.dot_general(qs, km_lo, NT_DIMS, preferred_element_type=F32))
    sel = _moba_select(gate, i, n_blocks)

    def scores(start):
        return lax.dot_general(qs, k_ref[0, pl.ds(start, MOBA_BLOCK), :], NT_DIMS, preferred_element_type=F32)

    own_start = pl.multiple_of(i * MOBA_BLOCK, MOBA_BLOCK)
    s = scores(own_start)
    qpos = lax.broadcasted_iota(jnp.int32, s.shape, 0) & (tq - 1)
    kpos = lax.broadcasted_iota(jnp.int32, s.shape, 1)
    s = jnp.where(kpos <= qpos, s, NEG)
    m = jnp.max(s, axis=1, keepdims=True)
    p = jnp.exp(s - m)
    m_sc[...] = m
    l_sc[...] = jnp.sum(p, axis=1, keepdims=True)
    acc_sc[...] = jnp.dot(p.astype(BF16), v_ref[0, pl.ds(own_start, MOBA_BLOCK), :], preferred_element_type=F32)

    for j in range(n_blocks - 1):
        @pl.when(j < i)
        def _(j=j):
            s = jnp.where(sel[j], scores(j * MOBA_BLOCK), NEG)
            m_old = m_sc[...]
            m_new = jnp.maximum(m_old, jnp.max(s, axis=1, keepdims=True))
            alpha = jnp.exp(m_old - m_new)
            p = jnp.exp(s - m_new)
            l_sc[...] = alpha * l_sc[...] + jnp.sum(p, axis=1, keepdims=True)
            acc_sc[...] = alpha * acc_sc[...] + jnp.dot(
                p.astype(BF16), v_ref[0, j * MOBA_BLOCK:(j + 1) * MOBA_BLOCK, :], preferred_element_type=F32)
            m_sc[...] = m_new

    out = acc_sc[...] / l_sc[...]
    o = jnp.zeros((tq, LANES), F32)
    for g in range(g_n):
        o = jnp.where(lane // HEAD_DIM == g, out[g * tq:(g + 1) * tq], o)
    o_ref[0] = o.astype(o_ref.dtype)


def _prompt_attention(q, k, v, kmean):
    b, l, _ = q.shape
    n_blocks = l // MOBA_BLOCK
    rows = HEADS_PER_LANE_BLOCK * MOBA_BLOCK
    return pl.pallas_call(
        functools.partial(_prompt_attn_kernel, n_blocks=n_blocks),
        grid=(b, ATTN_W // LANES, n_blocks),
        in_specs=[
            pl.BlockSpec((1, MOBA_BLOCK, LANES), lambda bi, c, i: (bi, i, c)),
            pl.BlockSpec((1, l, LANES), lambda bi, c, i: (bi, 0, c)),
            pl.BlockSpec((1, l, LANES), lambda bi, c, i: (bi, 0, c)),
            pl.BlockSpec((1, n_blocks, LANES), lambda bi, c, i: (bi, 0, c)),
        ],
        out_specs=pl.BlockSpec((1, MOBA_BLOCK, LANES), lambda bi, c, i: (bi, i, c)),
        out_shape=jax.ShapeDtypeStruct((b, l, ATTN_W), BF16),
        scratch_shapes=[pltpu.VMEM((rows, 1), F32), pltpu.VMEM((rows, 1), F32), pltpu.VMEM((rows, LANES), F32)],
        compiler_params=pltpu.CompilerParams(
            dimension_semantics=("arbitrary", "arbitrary", "arbitrary"), vmem_limit_bytes=VMEM_LIMIT),
        name="prompt_attn",
    )(q, k, v, kmean)


def _sample_attn_kernel(pt_ref, qt_ref, knt_ref, vnt_ref, ck_hbm, cv_hbm, o_ref,
                        kbuf, vbuf, qb_sc, s_sc, sem_k, sem_v, *, n_pages, page):
    b = pl.program_id(0)
    slot = b % 2
    ppb = MOBA_BLOCK // page
    n_past = n_pages // ppb

    def k_copy(seq, sl, p):
        return pltpu.make_async_copy(ck_hbm.at[pt_ref[seq, p]], kbuf.at[sl, p], sem_k.at[sl])

    @pl.when(b == 0)
    def _():
        o_ref[...] = jnp.zeros_like(o_ref)
        for p in range(n_pages):
            k_copy(0, 0, p).start()

    for p in range(n_pages):
        k_copy(b, slot, p).wait()

    @pl.when(b + 1 < pl.num_programs(0))
    def _():
        for p in range(n_pages):
            k_copy(b + 1, 1 - slot, p).start()

    def column(ref):
        a = ref[...]
        seq_id = lax.broadcasted_iota(jnp.int32, a.shape, 1)
        return jnp.sum(jnp.where(seq_id == b, a, 0.0), axis=1, keepdims=True)

    q_col, kn_col, vn_col = column(qt_ref), column(knt_ref), column(vnt_ref)
    qb_sc[...] = jnp.broadcast_to(q_col, qb_sc.shape)

    def page_scores(kt):
        return jnp.sum((kt * qb_sc[...]).reshape(N_HEADS, HEAD_DIM, page), axis=1)

    def score_body(p, carry):
        s_sc[p] = page_scores(kbuf[slot, p])
        return carry
    lax.fori_loop(0, n_pages, score_body, 0)

    pos_w = lax.broadcasted_iota(jnp.int32, (ATTN_W, page), 1)
    s_self = page_scores(jnp.where(pos_w == 0, kn_col, 0.0))

    hl = lax.broadcasted_iota(jnp.int32, (N_HEADS, LANES), 1)
    cand = jnp.where(hl == n_past, NEG * SCALE, -jnp.inf).astype(F32)
    for j in range(n_past):
        blk = s_sc[j * ppb]
        for r in range(1, ppb):
            blk = blk + s_sc[j * ppb + r]
        cand = jnp.where(hl == j, jnp.sum(blk, axis=1, keepdims=True) * (1.0 / MOBA_BLOCK), cand)
    picks = []
    for _ in range(MOBA_TOPK):
        best = jnp.max(cand, axis=1, keepdims=True)
        idx = jnp.min(jnp.where(cand == best, hl, LANES), axis=1, keepdims=True)
        picks.append(idx)
        cand = jnp.where(hl == idx, -jnp.inf, cand)

    blocks = [[picks[t][h, 0] for t in range(MOBA_TOPK)] for h in range(N_HEADS)]
    clamped = [[jnp.minimum(blk, n_past - 1) for blk in row] for row in blocks]

    def v_copy(h, t, r):
        phys = pt_ref[b, clamped[h][t] * ppb + r]
        return pltpu.make_async_copy(cv_hbm.at[phys, pl.ds(h * HEAD_DIM, HEAD_DIM), :],
                                     vbuf.at[(h * MOBA_TOPK + t) * ppb + r], sem_v)

    for h in range(N_HEADS):
        for t in range(MOBA_TOPK):
            for r in range(ppb):
                v_copy(h, t, r).start()

    s_all = s_sc[...]
    blk_id = lax.broadcasted_iota(jnp.int32, s_all.shape, 0) // ppb
    chosen = blk_id == picks[0][None]
    for t in range(1, MOBA_TOPK):
        chosen = chosen | (blk_id == picks[t][None])
    s_all = jnp.where(chosen, s_all, NEG)
    s_self = jnp.where(hl == 0, s_self, NEG)
    m = jnp.max(jnp.maximum(jnp.max(s_all, axis=0), s_self), axis=1, keepdims=True)
    p_all = jnp.exp(s_all - m[None])
    p_self = jnp.exp(s_self - m)
    denom = jnp.sum(jnp.sum(p_all, axis=0) + p_self, axis=1, keepdims=True)
    s_sc[...] = p_all / denom[None]
    p_self = p_self / denom

    for h in range(N_HEADS):
        for t in range(MOBA_TOPK):
            for r in range(ppb):
                v_copy(h, t, r).wait()

    pos_h = lax.broadcasted_iota(jnp.int32, (HEAD_DIM, page), 1)
    cols = []
    for h in range(N_HEADS):
        rows = slice(h * HEAD_DIM, (h + 1) * HEAD_DIM)
        acc = p_self[h:h + 1, :] * jnp.where(pos_h == 0, vn_col[rows], 0.0)
        for t in range(MOBA_TOPK):
            for r in range(ppb):
                p_row = s_sc[clamped[h][t] * ppb + r, pl.ds(h, 1), :]
                p_row = jnp.where(blocks[h][t] < n_past, p_row, 0.0)
                acc = acc + p_row * vbuf[(h * MOBA_TOPK + t) * ppb + r]
        cols.append(jnp.sum(acc, axis=1, keepdims=True))
    o_col = jnp.concatenate(cols, axis=0)
    seq_id = lax.broadcasted_iota(jnp.int32, o_ref.shape, 1)
    o_ref[...] = jnp.where(seq_id == b, o_col, o_ref[...])


def _sample_attention(q, k_new, v_new, cache_k, cache_v, page_table):
    s, n_pages = page_table.shape
    n_phys, page = cache_k.shape[0], cache_k.shape[1]
    assert page == LANES and MOBA_BLOCK % page == 0 and (n_pages * page) % MOBA_BLOCK == 0
    n_past = n_pages * page // MOBA_BLOCK
    assert 1 <= n_past < LANES and s <= LANES
    ck = cache_k.transpose(0, 2, 3, 1).reshape(n_phys, ATTN_W, page)
    cv = cache_v.transpose(0, 2, 3, 1).reshape(n_phys, ATTN_W, page)
    n_vbuf = N_HEADS * MOBA_TOPK * (MOBA_BLOCK // page)
    tok = pl.BlockSpec((ATTN_W, s), lambda b, pt: (0, 0))
    grid_spec = pltpu.PrefetchScalarGridSpec(
        num_scalar_prefetch=1,
        grid=(s,),
        in_specs=[tok, tok, tok, pl.BlockSpec(memory_space=pl.ANY), pl.BlockSpec(memory_space=pl.ANY)],
        out_specs=tok,
        scratch_shapes=[
            pltpu.VMEM((2, n_pages, ATTN_W, page), F32),
            pltpu.VMEM((n_vbuf, HEAD_DIM, page), F32),
            pltpu.VMEM((ATTN_W, page), F32),
            pltpu.VMEM((n_pages, N_HEADS, page), F32),
            pltpu.SemaphoreType.DMA((2,)),
            pltpu.SemaphoreType.DMA(()),
        ],
    )
    out_t = pl.pallas_call(
        functools.partial(_sample_attn_kernel, n_pages=n_pages, page=page),
        grid_spec=grid_spec,
        out_shape=jax.ShapeDtypeStruct((ATTN_W, s), F32),
        compiler_params=pltpu.CompilerParams(dimension_semantics=("arbitrary",), vmem_limit_bytes=VMEM_LIMIT),
        name="sample_attn",
    )(page_table, q.T, k_new.T, v_new.T, ck, cv)
    return out_t.T


def _merge_kernel(x_ref, oa_ref, u_ref, vv_ref, sga_ref, sgb_ref, ws_ref, bs_ref, wpa_ref, wpb_ref, wo_ref,
                  x1_ref, *, single_position):
    tm = x_ref.shape[0]
    if single_position:
        s = vv_ref[...].astype(F32) * ws_ref[...] + bs_ref[...]
    else:
        t_idx = lax.broadcasted_iota(jnp.int32, (SGU_CHUNK, SGU_CHUNK), 0)
        s_idx = lax.broadcasted_iota(jnp.int32, (SGU_CHUNK, SGU_CHUNK), 1)
        ws = [jnp.where(s_idx <= t_idx, ws_ref[g], 0.0).astype(BF16) for g in range(SGU_GROUPS)]
        lane = lax.broadcasted_iota(jnp.int32, (SGU_CHUNK, LANES), 1)
        groups_per_block = LANES // SGU_GROUP_W
        chunks = []
        for c in range(tm // SGU_CHUNK):
            parts = []
            for blk in range(SGU_W // LANES):
                vp = vv_ref[c * SGU_CHUNK:(c + 1) * SGU_CHUNK, blk * LANES:(blk + 1) * LANES]
                sp = jnp.zeros((SGU_CHUNK, LANES), F32)
                for g in range(groups_per_block):
                    sg = jnp.dot(ws[blk * groups_per_block + g], vp, preferred_element_type=F32)
                    sp = jnp.where(lane // SGU_GROUP_W == g, sg, sp)
                parts.append(sp)
            chunks.append(jnp.concatenate(parts, axis=1) + bs_ref[...])
        s = jnp.concatenate(chunks, axis=0)
    ob = (u_ref[...].astype(F32) * s).astype(BF16)
    ba = jnp.dot(oa_ref[...], wpa_ref[...], preferred_element_type=F32)
    bb = jnp.dot(ob, wpb_ref[...], preferred_element_type=F32)
    merged = (sga_ref[...].astype(F32) * ba + sgb_ref[...].astype(F32) * bb).astype(BF16)
    x1_ref[...] = x_ref[...] + jnp.dot(merged, wo_ref[...], preferred_element_type=F32)


def _merge(x2d, o_a, u, vv, sga, sgb, w_s, b_s, w_pa_bf, w_pb_bf, w_o_bf, *, tm, single_position):
    m, d = x2d.shape
    if single_position:
        ws_in = jnp.repeat(w_s[:, 0, 0], SGU_GROUP_W).reshape(1, SGU_W)
        bs_in = jnp.repeat(b_s[:, 0], SGU_GROUP_W).reshape(1, SGU_W)
    else:
        ws_in = w_s
        bs_in = jnp.repeat(b_s.T, SGU_GROUP_W, axis=1)
    row = lambda w: pl.BlockSpec((tm, w), lambda r: (r, 0))
    return pl.pallas_call(
        functools.partial(_merge_kernel, single_position=single_position),
        grid=(m // tm,),
        in_specs=[row(d), row(ATTN_W), row(SGU_W), row(SGU_W), row(d), row(d),
                  _const_spec(ws_in.shape), _const_spec(bs_in.shape),
                  _const_spec(w_pa_bf.shape), _const_spec(w_pb_bf.shape), _const_spec(w_o_bf.shape)],
        out_specs=row(d),
        out_shape=jax.ShapeDtypeStruct((m, d), F32),
        compiler_params=pltpu.CompilerParams(dimension_semantics=("arbitrary",), vmem_limit_bytes=VMEM_LIMIT),
        name="merge",
    )(x2d, o_a, u, vv, sga, sgb, ws_in, bs_in, w_pa_bf, w_pb_bf, w_o_bf)


FFN_CHUNK = 1024


def _ffn_kernel(x_ref, g_ref, wup_ref, wdn_ref, gf_ref, y_ref):
    x = x_ref[...]
    h = _rmsnorm(x, g_ref[...]).astype(BF16)
    acc = x
    for c in range(wup_ref.shape[1] // FFN_CHUNK):
        a = jnp.dot(h, wup_ref[:, c * FFN_CHUNK:(c + 1) * FFN_CHUNK], preferred_element_type=F32)
        a = jnp.square(jnp.maximum(a, 0.0)).astype(BF16)
        acc = acc + jnp.dot(a, wdn_ref[c * FFN_CHUNK:(c + 1) * FFN_CHUNK, :], preferred_element_type=F32)
    y_ref[...] = _rmsnorm(acc, gf_ref[...])


def _ffn(x2d, g_ffn, w_up_bf, w_down_bf, g_final, *, tm):
    m, d = x2d.shape
    row = pl.BlockSpec((tm, d), lambda r: (r, 0))
    return pl.pallas_call(
        _ffn_kernel,
        grid=(m // tm,),
        in_specs=[row, _const_spec((1, d)), _const_spec(w_up_bf.shape), _const_spec(w_down_bf.shape),
                  _const_spec((1, d))],
        out_specs=row,
        out_shape=jax.ShapeDtypeStruct((m, d), F32),
        compiler_params=pltpu.CompilerParams(dimension_semantics=("arbitrary",), vmem_limit_bytes=VMEM_LIMIT),
        name="ffn",
    )(x2d, g_ffn.reshape(1, d), w_up_bf, w_down_bf, g_final.reshape(1, d))


def _rope_tables(pos):
    half = HEAD_DIM // 2
    inv = ROPE_THETA ** (-2.0 * jnp.arange(half, dtype=F32) / HEAD_DIM)
    ang = pos.astype(F32)[:, None] * inv[None, :]
    cos = jnp.tile(jnp.cos(ang), (1, 2 * N_HEADS))
    sin = jnp.tile(jnp.concatenate([-jnp.sin(ang), jnp.sin(ang)], axis=1), (1, N_HEADS))
    return cos, sin


def kernel(x_prompt, x_sample, cache_k, cache_v, page_table, g_attn, w_in, w_pa, w_pb, w_o, g_v, w_s, b_s,
           g_ffn, w_up, w_down, g_final):
    depth = w_in.shape[0]
    assert depth == 1, "the final norm is fused into the last layer's MLP kernel"
    bsz, seq, d = x_prompt.shape
    n_dec, dec_seq, _ = x_sample.shape
    assert dec_seq == 1 and seq % MOBA_BLOCK == 0
    past_len = page_table.shape[1] * cache_k.shape[2]
    n_chunk_rows = seq - ((seq - 1) // SGU_CHUNK) * SGU_CHUNK
    assert n_chunk_rows == SGU_CHUNK

    cos_p, sin_p = _rope_tables(jnp.arange(seq, dtype=jnp.int32))
    cos_s, sin_s = _rope_tables(jnp.full((n_dec,), past_len, dtype=jnp.int32))

    xp = x_prompt.reshape(bsz * seq, d)
    xs = x_sample.reshape(n_dec, d)
    l = 0
    w_in_bf, w_pa_bf, w_pb_bf, w_o_bf = (w[l].astype(BF16) for w in (w_in, w_pa, w_pb, w_o))
    w_up_bf, w_down_bf = w_up[l].astype(BF16), w_down[l].astype(BF16)

    tm_p = MOBA_BLOCK
    (q, k_p, v_p, kb, vb, u, vv, sga, sgb, sguv_p, kmean) = _inproj(
        xp, g_attn[l], w_in_bf, cos_p, sin_p, g_v[l], tm=tm_p, seq_tiles=seq // tm_p, q_dtype=BF16, emit_kmean=True)
    shp = (bsz, seq, ATTN_W)
    o_a = _prompt_attention(q.reshape(shp), kb.reshape(shp), vb.reshape(shp),
                            kmean.reshape(bsz, seq // MOBA_BLOCK, ATTN_W)).reshape(bsz * seq, ATTN_W)
    x1 = _merge(xp, o_a, u, vv, sga, sgb, w_s[l], b_s[l], w_pa_bf, w_pb_bf, w_o_bf, tm=tm_p, single_position=False)
    y_p = _ffn(x1, g_ffn[l], w_up_bf, w_down_bf, g_final, tm=512)

    (q_s, k_s, v_s, _, _, u_s, vv_s, sga_s, sgb_s, sguv_s) = _inproj(
        xs, g_attn[l], w_in_bf, cos_s, sin_s, g_v[l], tm=n_dec, seq_tiles=1, q_dtype=F32, emit_kmean=False)
    o_as = _sample_attention(q_s, k_s, v_s, cache_k[l], cache_v[l], page_table).astype(BF16)
    x1_s = _merge(xs, o_as, u_s, vv_s, sga_s, sgb_s, w_s[l], b_s[l], w_pa_bf, w_pb_bf, w_o_bf,
                  tm=n_dec, single_position=True)
    y_s = _ffn(x1_s, g_ffn[l], w_up_bf, w_down_bf, g_final, tm=n_dec)

    return (y_p.reshape(bsz, seq, d),
            y_s.reshape(n_dec, 1, d),
            k_p.reshape(1, bsz, seq, N_HEADS, HEAD_DIM),
            v_p.reshape(1, bsz, seq, N_HEADS, HEAD_DIM),
            sguv_p.reshape(1, bsz, SGU_CHUNK, SGU_W),
            k_s.reshape(1, n_dec, 1, N_HEADS, HEAD_DIM),
            v_s.reshape(1, n_dec, 1, N_HEADS, HEAD_DIM),
            sguv_s.reshape(1, n_dec, 1, SGU_W))
```

```python
import functools

import jax
import jax.numpy as jnp
from jax import lax
from jax.experimental import pallas as pl
from jax.experimental.pallas import tpu as pltpu

N_HEADS = 8
HEAD_DIM = 64
ATTN_W = N_HEADS * HEAD_DIM
MOBA_BLOCK = 256
MOBA_TOPK = 3
SGU_GROUPS = 8
SGU_W = 512
SGU_GROUP_W = SGU_W // SGU_GROUPS
SGU_CHUNK = 128
ROPE_THETA = 10000.0
EPS = 1e-6
NEG = -1e30
SCALE = HEAD_DIM ** -0.5
LOG2E = 1.4426950408889634
PROMPT_Q_SCALE = SCALE * LOG2E

LANES = 128
HEADS_PER_LANE_BLOCK = LANES // HEAD_DIM
VMEM_LIMIT = 56 * 1024 * 1024

F32 = jnp.float32
BF16 = jnp.bfloat16
NT_DIMS = (((1,), (1,)), ((), ()))


def _rmsnorm(x, g):
    return x * lax.rsqrt(jnp.mean(x * x, axis=-1, keepdims=True) + EPS) * g


def _const_spec(shape):
    return pl.BlockSpec(shape, lambda *_: (0,) * len(shape), pipeline_mode=pl.Buffered(1))


def _rope(t, cos, sin_signed):
    lane = lax.broadcasted_iota(jnp.int32, (t.shape[0], LANES), 1)
    first_half = (lane & (HEAD_DIM - 1)) < HEAD_DIM // 2
    outs = []
    for c in range(t.shape[1] // LANES):
        sl = slice(c * LANES, (c + 1) * LANES)
        tc = t[:, sl]
        partner = jnp.where(first_half, pltpu.roll(tc, LANES - HEAD_DIM // 2, 1),
                            pltpu.roll(tc, HEAD_DIM // 2, 1))
        outs.append(tc * cos[:, sl] + partner * sin_signed[:, sl])
    return jnp.concatenate(outs, axis=1)


def _inproj_kernel(x_ref, g_ref, w_ref, cos_ref, sin_ref, gv_ref,
                   q_ref, kf_ref, vf_ref, kb_ref, vb_ref, u_ref, vv_ref, sga_ref, sgb_ref, sguv_ref,
                   *maybe_kmean_ref, q_scale):
    d_model = x_ref.shape[1]
    h = _rmsnorm(x_ref[...], g_ref[...]).astype(BF16)

    def proj(lo, width):
        return jnp.dot(h, w_ref[:, lo:lo + width], preferred_element_type=F32)

    cos = cos_ref[...]
    sin = sin_ref[...]
    q = _rope(proj(0, ATTN_W), cos, sin) * q_scale
    if len(q_ref.shape) == 3:
        q_ref[0] = q.T.astype(q_ref.dtype)
    else:
        q_ref[...] = q.astype(q_ref.dtype)
    k = _rope(proj(ATTN_W, ATTN_W), cos, sin)
    kf_ref[0] = k.T
    kb_ref[...] = k.astype(BF16)
    if maybe_kmean_ref:
        maybe_kmean_ref[0][0] = jnp.mean(k, axis=0, keepdims=True)
    v = proj(2 * ATTN_W, ATTN_W)
    vf_ref[0] = v.T
    vb_ref[...] = v.astype(BF16)
    base = 3 * ATTN_W
    u_ref[...] = jax.nn.gelu(proj(base, SGU_W)).astype(BF16)
    vv = _rmsnorm(jax.nn.gelu(proj(base + SGU_W, SGU_W)), gv_ref[...])
    vv_ref[...] = vv.astype(BF16)
    sguv_ref[...] = vv[vv.shape[0] - sguv_ref.shape[0]:, :]
    base += 2 * SGU_W
    sga_ref[...] = jax.nn.sigmoid(proj(base, d_model)).astype(BF16)
    sgb_ref[...] = jax.nn.sigmoid(proj(base + d_model, d_model)).astype(BF16)


def _inproj(x2d, g_attn, w_in_bf, cos_tab, sin_tab, g_v, *, tm, seq_tiles, channel_major_q, q_scale, emit_kmean):
    m, d = x2d.shape
    n_tiles = m // tm
    n_seq = n_tiles // seq_tiles
    keep = min(SGU_CHUNK, tm)
    row = lambda w: pl.BlockSpec((tm, w), lambda r: (r, 0))
    tab = pl.BlockSpec((tm, ATTN_W), lambda r: (r % seq_tiles, 0))
    chan = pl.BlockSpec((1, ATTN_W, tm), lambda r: (r // seq_tiles, 0, r % seq_tiles))
    chan_shape = jax.ShapeDtypeStruct((n_seq, ATTN_W, seq_tiles * tm), F32)
    out_shape = [
        chan_shape if channel_major_q else jax.ShapeDtypeStruct((m, ATTN_W), BF16),
        chan_shape,
        chan_shape,
        jax.ShapeDtypeStruct((m, ATTN_W), BF16),
        jax.ShapeDtypeStruct((m, ATTN_W), BF16),
        jax.ShapeDtypeStruct((m, SGU_W), BF16),
        jax.ShapeDtypeStruct((m, SGU_W), BF16),
        jax.ShapeDtypeStruct((m, d), BF16),
        jax.ShapeDtypeStruct((m, d), BF16),
        jax.ShapeDtypeStruct((n_seq * keep, SGU_W), F32),
    ]
    out_specs = [chan if channel_major_q else row(ATTN_W), chan, chan] + [row(ATTN_W)] * 2 + [
        row(SGU_W)] * 2 + [row(d)] * 2 + [pl.BlockSpec((keep, SGU_W), lambda r: (r // seq_tiles, 0))]
    if emit_kmean:
        out_shape.append(jax.ShapeDtypeStruct((n_tiles, 1, ATTN_W), F32))
        out_specs.append(pl.BlockSpec((1, 1, ATTN_W), lambda r: (r, 0, 0)))
    return pl.pallas_call(
        functools.partial(_inproj_kernel, q_scale=q_scale),
        grid=(n_tiles,),
        in_specs=[row(d), _const_spec((1, d)), _const_spec(w_in_bf.shape), tab, tab, _const_spec((1, SGU_W))],
        out_specs=out_specs,
        out_shape=out_shape,
        compiler_params=pltpu.CompilerParams(dimension_semantics=("arbitrary",), vmem_limit_bytes=VMEM_LIMIT),
        name="inproj",
    )(x2d, g_attn.reshape(1, d), w_in_bf, cos_tab, sin_tab, g_v.reshape(1, SGU_W))


def _moba_bias(gate, tile):
    col = lax.broadcasted_iota(jnp.int32, gate.shape, 1).astype(F32)
    neg = NEG * PROMPT_Q_SCALE
    gate = jnp.where(col < tile, gate, neg)
    if tile <= MOBA_TOPK:
        return jnp.where(gate >= neg, 0.0, NEG)
    cand = gate
    for t in range(MOBA_TOPK):
        best = jnp.max(cand, axis=1, keepdims=True)
        idx = jnp.min(jnp.where(cand == best, col, float(LANES)), axis=1, keepdims=True)
        if t + 1 < MOBA_TOPK:
            cand = jnp.where(col == idx, -jnp.inf, cand)
    bias = jnp.where(gate > best, 0.0, jnp.where(gate == best, jnp.where(col <= idx, 0.0, NEG), NEG))
    return jnp.where(col >= tile, 0.0, bias)


def _attn_tile(q_ref, k_ref, v_ref, km_ref, e_ref, o_ref, s_sc, tile):
    tq = MOBA_BLOCK
    g_n = HEADS_PER_LANE_BLOCK
    q = q_ref[0]
    lane = lax.broadcasted_iota(jnp.int32, (tq, LANES), 1)
    qs = jnp.concatenate([jnp.where(lane // HEAD_DIM == g, q, jnp.zeros_like(q)) for g in range(g_n)], axis=0)

    if tile == 0:
        lhs = qs
    else:
        km = km_ref[0]
        km = jnp.concatenate([km, jnp.zeros((LANES - km.shape[0], LANES), F32)], axis=0)
        km_hi = km.astype(BF16)
        km_lo = (km - km_hi.astype(F32)).astype(BF16)
        gate = (lax.dot_general(qs, km_hi, NT_DIMS, preferred_element_type=F32)
                + lax.dot_general(qs, km_lo, NT_DIMS, preferred_element_type=F32))
        lhs = jnp.concatenate([qs, _moba_bias(gate, tile).astype(BF16)], axis=1)

    rows = g_n * tq
    qpos = lax.broadcasted_iota(jnp.int32, (rows, MOBA_BLOCK), 0) & (tq - 1)
    kpos = lax.broadcasted_iota(jnp.int32, (rows, MOBA_BLOCK), 1)
    run_max = None
    for j in range(tile + 1):
        keys = k_ref[0, j * MOBA_BLOCK:(j + 1) * MOBA_BLOCK, :]
        if tile > 0:
            keys = jnp.concatenate([keys, e_ref[j * MOBA_BLOCK:(j + 1) * MOBA_BLOCK, :]], axis=1)
        s = lax.dot_general(lhs, keys, NT_DIMS, preferred_element_type=F32)
        if j == tile:
            s = jnp.where(kpos <= qpos, s, NEG)
        s_sc[j] = s
        run_max = s if run_max is None else jnp.maximum(run_max, s)
    m = jnp.max(run_max, axis=1, keepdims=True)

    acc = None
    run_sum = None
    for j in range(tile + 1):
        p = jnp.exp2(s_sc[j] - m)
        run_sum = p if run_sum is None else run_sum + p
        pv = jnp.dot(p.astype(BF16), v_ref[0, j * MOBA_BLOCK:(j + 1) * MOBA_BLOCK, :], preferred_element_type=F32)
        acc = pv if acc is None else acc + pv
    out = acc / jnp.sum(run_sum, axis=1, keepdims=True)
    o = jnp.zeros((tq, LANES), F32)
    for g in range(g_n):
        o = jnp.where(lane // HEAD_DIM == g, out[g * tq:(g + 1) * tq], o)
    o_ref[0] = o.astype(o_ref.dtype)


def _prompt_attn_kernel(q_ref, k_ref, v_ref, km_ref, e_ref, o_ref, s_sc, *, n_blocks):
    i = pl.program_id(2)
    for tile in range(n_blocks):
        @pl.when(i == tile)
        def _(tile=tile):
            _attn_tile(q_ref, k_ref, v_ref, km_ref, e_ref, o_ref, s_sc, tile)


def _prompt_attention(q, k, v, kmean):
    b, l, _ = q.shape
    n_blocks = l // MOBA_BLOCK
    assert n_blocks <= LANES
    rows = HEADS_PER_LANE_BLOCK * MOBA_BLOCK
    block_of_key = (jnp.arange(l, dtype=jnp.int32) // MOBA_BLOCK)[:, None]
    onehot = (block_of_key == jnp.arange(LANES, dtype=jnp.int32)[None, :]).astype(BF16)
    return pl.pallas_call(
        functools.partial(_prompt_attn_kernel, n_blocks=n_blocks),
        grid=(b, ATTN_W // LANES, n_blocks),
        in_specs=[
            pl.BlockSpec((1, MOBA_BLOCK, LANES), lambda bi, c, i: (bi, i, c)),
            pl.BlockSpec((1, l, LANES), lambda bi, c, i: (bi, 0, c)),
            pl.BlockSpec((1, l, LANES), lambda bi, c, i: (bi, 0, c)),
            pl.BlockSpec((1, n_blocks, LANES), lambda bi, c, i: (bi, 0, c)),
            _const_spec((l, LANES)),
        ],
        out_specs=pl.BlockSpec((1, MOBA_BLOCK, LANES), lambda bi, c, i: (bi, i, c)),
        out_shape=jax.ShapeDtypeStruct((b, l, ATTN_W), BF16),
        scratch_shapes=[pltpu.VMEM((n_blocks, rows, MOBA_BLOCK), F32)],
        compiler_params=pltpu.CompilerParams(
            dimension_semantics=("arbitrary", "arbitrary", "arbitrary"), vmem_limit_bytes=VMEM_LIMIT),
        name="prompt_attn",
    )(q, k, v, kmean, onehot)


def _sample_attn_kernel(pt_ref, qt_ref, knt_ref, vnt_ref, ck_hbm, cv_hbm, o_ref,
                        kbuf, vbuf, qb_sc, s_sc, sem_k, sem_v, *, n_pages, page):
    b = pl.program_id(0)
    slot = b % 2
    ppb = MOBA_BLOCK // page
    n_past = n_pages // ppb

    def k_copy(seq, sl, p):
        return pltpu.make_async_copy(ck_hbm.at[pt_ref[seq, p]], kbuf.at[sl, p], sem_k.at[sl])

    @pl.when(b == 0)
    def _():
        o_ref[...] = jnp.zeros_like(o_ref)
        for p in range(n_pages):
            k_copy(0, 0, p).start()

    for p in range(n_pages):
        k_copy(b, slot, p).wait()

    @pl.when(b + 1 < pl.num_programs(0))
    def _():
        for p in range(n_pages):
            k_copy(b + 1, 1 - slot, p).start()

    def column(ref):
        a = ref[...]
        seq_id = lax.broadcasted_iota(jnp.int32, a.shape, 1)
        return jnp.sum(jnp.where(seq_id == b, a, 0.0), axis=1, keepdims=True)

    q_col, kn_col, vn_col = column(qt_ref), column(knt_ref), column(vnt_ref)
    qb_sc[...] = jnp.broadcast_to(q_col, qb_sc.shape)

    def page_scores(kt):
        return jnp.sum((kt * qb_sc[...]).reshape(N_HEADS, HEAD_DIM, page), axis=1)

    def score_body(p, carry):
        s_sc[p] = page_scores(kbuf[slot, p])
        return carry
    lax.fori_loop(0, n_pages, score_body, 0)

    pos_w = lax.broadcasted_iota(jnp.int32, (ATTN_W, page), 1)
    s_self = page_scores(jnp.where(pos_w == 0, kn_col, 0.0))

    hl = lax.broadcasted_iota(jnp.int32, (N_HEADS, LANES), 1)
    cand = jnp.where(hl == n_past, NEG * SCALE, -jnp.inf).astype(F32)
    for j in range(n_past):
        blk = s_sc[j * ppb]
        for r in range(1, ppb):
            blk = blk + s_sc[j * ppb + r]
        cand = jnp.where(hl == j, jnp.sum(blk, axis=1, keepdims=True) * (1.0 / MOBA_BLOCK), cand)
    picks = []
    for _ in range(MOBA_TOPK):
        best = jnp.max(cand, axis=1, keepdims=True)
        idx = jnp.min(jnp.where(cand == best, hl, LANES), axis=1, keepdims=True)
        picks.append(idx)
        cand = jnp.where(hl == idx, -jnp.inf, cand)

    blocks = [[picks[t][h, 0] for t in range(MOBA_TOPK)] for h in range(N_HEADS)]
    clamped = [[jnp.minimum(blk, n_past - 1) for blk in row] for row in blocks]

    def v_copy(h, t, r):
        phys = pt_ref[b, clamped[h][t] * ppb + r]
        return pltpu.make_async_copy(cv_hbm.at[phys, pl.ds(h * HEAD_DIM, HEAD_DIM), :],
                                     vbuf.at[(h * MOBA_TOPK + t) * ppb + r], sem_v)

    for h in range(N_HEADS):
        for t in range(MOBA_TOPK):
            for r in range(ppb):
                v_copy(h, t, r).start(priority=1)

    s_all = s_sc[...]
    blk_id = lax.broadcasted_iota(jnp.int32, s_all.shape, 0) // ppb
    chosen = blk_id == picks[0][None]
    for t in range(1, MOBA_TOPK):
        chosen = chosen | (blk_id == picks[t][None])
    s_all = jnp.where(chosen, s_all, NEG)
    s_self = jnp.where(hl == 0, s_self, NEG)
    m = jnp.max(jnp.maximum(jnp.max(s_all, axis=0), s_self), axis=1, keepdims=True)
    p_all = jnp.exp(s_all - m[None])
    p_self = jnp.exp(s_self - m)
    denom = jnp.sum(jnp.sum(p_all, axis=0) + p_self, axis=1, keepdims=True)
    s_sc[...] = p_all / denom[None]
    p_self = p_self / denom

    for h in range(N_HEADS):
        for t in range(MOBA_TOPK):
            for r in range(ppb):
                v_copy(h, t, r).wait()

    pos_h = lax.broadcasted_iota(jnp.int32, (HEAD_DIM, page), 1)
    cols = []
    for h in range(N_HEADS):
        rows = slice(h * HEAD_DIM, (h + 1) * HEAD_DIM)
        acc = p_self[h:h + 1, :] * jnp.where(pos_h == 0, vn_col[rows], 0.0)
        for t in range(MOBA_TOPK):
            for r in range(ppb):
                p_row = s_sc[clamped[h][t] * ppb + r, pl.ds(h, 1), :]
                p_row = jnp.where(blocks[h][t] < n_past, p_row, 0.0)
                acc = acc + p_row * vbuf[(h * MOBA_TOPK + t) * ppb + r]
        cols.append(jnp.sum(acc, axis=1, keepdims=True))
    o_col = jnp.concatenate(cols, axis=0)
    seq_id = lax.broadcasted_iota(jnp.int32, o_ref.shape, 1)
    o_ref[...] = jnp.where(seq_id == b, o_col, o_ref[...])


def _sample_attention(q_t, k_new_t, v_new_t, cache_k, cache_v, page_table):
    s, n_pages = page_table.shape
    n_phys, page = cache_k.shape[0], cache_k.shape[1]
    assert page == LANES and MOBA_BLOCK % page == 0 and (n_pages * page) % MOBA_BLOCK == 0
    n_past = n_pages * page // MOBA_BLOCK
    assert 1 <= n_past < LANES and s <= LANES
    ck = cache_k.transpose(0, 2, 3, 1).reshape(n_phys, ATTN_W, page)
    cv = cache_v.transpose(0, 2, 3, 1).reshape(n_phys, ATTN_W, page)
    n_vbuf = N_HEADS * MOBA_TOPK * (MOBA_BLOCK // page)
    tok = pl.BlockSpec((ATTN_W, s), lambda b, pt: (0, 0))
    grid_spec = pltpu.PrefetchScalarGridSpec(
        num_scalar_prefetch=1,
        grid=(s,),
        in_specs=[tok, tok, tok, pl.BlockSpec(memory_space=pl.ANY), pl.BlockSpec(memory_space=pl.ANY)],
        out_specs=tok,
        scratch_shapes=[
            pltpu.VMEM((2, n_pages, ATTN_W, page), F32),
            pltpu.VMEM((n_vbuf, HEAD_DIM, page), F32),
            pltpu.VMEM((ATTN_W, page), F32),
            pltpu.VMEM((n_pages, N_HEADS, page), F32),
            pltpu.SemaphoreType.DMA((2,)),
            pltpu.SemaphoreType.DMA(()),
        ],
    )
    return pl.pallas_call(
        functools.partial(_sample_attn_kernel, n_pages=n_pages, page=page),
        grid_spec=grid_spec,
        out_shape=jax.ShapeDtypeStruct((ATTN_W, s), F32),
        compiler_params=pltpu.CompilerParams(dimension_semantics=("arbitrary",), vmem_limit_bytes=VMEM_LIMIT),
        name="sample_attn",
    )(page_table, q_t, k_new_t, v_new_t, ck, cv)


def _merge_kernel(x_ref, oa_ref, u_ref, vv_ref, sga_ref, sgb_ref, ws_ref, bs_ref, wpa_ref, wpb_ref, wo_ref,
                  x1_ref, *, single_position):
    tm = x_ref.shape[0]
    if single_position:
        s = vv_ref[...].astype(F32) * ws_ref[...] + bs_ref[...]
    else:
        t_idx = lax.broadcasted_iota(jnp.int32, (SGU_CHUNK, SGU_CHUNK), 0)
        s_idx = lax.broadcasted_iota(jnp.int32, (SGU_CHUNK, SGU_CHUNK), 1)
        ws = [jnp.where(s_idx <= t_idx, ws_ref[g], 0.0).astype(BF16) for g in range(SGU_GROUPS)]
        lane = lax.broadcasted_iota(jnp.int32, (SGU_CHUNK, LANES), 1)
        groups_per_block = LANES // SGU_GROUP_W
        chunks = []
        for c in range(tm // SGU_CHUNK):
            parts = []
            for blk in range(SGU_W // LANES):
                vp = vv_ref[c * SGU_CHUNK:(c + 1) * SGU_CHUNK, blk * LANES:(blk + 1) * LANES]
                sp = jnp.zeros((SGU_CHUNK, LANES), F32)
                for g in range(groups_per_block):
                    sg = jnp.dot(ws[blk * groups_per_block + g], vp, preferred_element_type=F32)
                    sp = jnp.where(lane // SGU_GROUP_W == g, sg, sp)
                parts.append(sp)
            chunks.append(jnp.concatenate(parts, axis=1) + bs_ref[...])
        s = jnp.concatenate(chunks, axis=0)
    ob = (u_ref[...].astype(F32) * s).astype(BF16)
    ba = jnp.dot(oa_ref[...], wpa_ref[...], preferred_element_type=F32)
    bb = jnp.dot(ob, wpb_ref[...], preferred_element_type=F32)
    merged = (sga_ref[...].astype(F32) * ba + sgb_ref[...].astype(F32) * bb).astype(BF16)
    x1_ref[...] = x_ref[...] + jnp.dot(merged, wo_ref[...], preferred_element_type=F32)


def _merge(x2d, o_a, u, vv, sga, sgb, w_s, b_s, w_pa_bf, w_pb_bf, w_o_bf, *, tm, single_position):
    m, d = x2d.shape
    if single_position:
        ws_in = jnp.repeat(w_s[:, 0, 0], SGU_GROUP_W).reshape(1, SGU_W)
        bs_in = jnp.repeat(b_s[:, 0], SGU_GROUP_W).reshape(1, SGU_W)
    else:
        ws_in = w_s
        bs_in = jnp.repeat(b_s.T, SGU_GROUP_W, axis=1)
    row = lambda w: pl.BlockSpec((tm, w), lambda r: (r, 0))
    return pl.pallas_call(
        functools.partial(_merge_kernel, single_position=single_position),
        grid=(m // tm,),
        in_specs=[row(d), row(ATTN_W), row(SGU_W), row(SGU_W), row(d), row(d),
                  _const_spec(ws_in.shape), _const_spec(bs_in.shape),
                  _const_spec(w_pa_bf.shape), _const_spec(w_pb_bf.shape), _const_spec(w_o_bf.shape)],
        out_specs=row(d),
        out_shape=jax.ShapeDtypeStruct((m, d), F32),
        compiler_params=pltpu.CompilerParams(dimension_semantics=("arbitrary",), vmem_limit_bytes=VMEM_LIMIT),
        name="merge",
    )(x2d, o_a, u, vv, sga, sgb, ws_in, bs_in, w_pa_bf, w_pb_bf, w_o_bf)


FFN_CHUNK = 1024


def _ffn_kernel(x_ref, g_ref, wup_ref, wdn_ref, gf_ref, y_ref):
    x = x_ref[...]
    h = _rmsnorm(x, g_ref[...]).astype(BF16)
    acc = x
    for c in range(wup_ref.shape[1] // FFN_CHUNK):
        a = jnp.dot(h, wup_ref[:, c * FFN_CHUNK:(c + 1) * FFN_CHUNK], preferred_element_type=F32)
        a = jnp.square(jnp.maximum(a, 0.0)).astype(BF16)
        acc = acc + jnp.dot(a, wdn_ref[c * FFN_CHUNK:(c + 1) * FFN_CHUNK, :], preferred_element_type=F32)
    y_ref[...] = _rmsnorm(acc, gf_ref[...])


def _ffn(x2d, g_ffn, w_up_bf, w_down_bf, g_final, *, tm):
    m, d = x2d.shape
    row = pl.BlockSpec((tm, d), lambda r: (r, 0))
    return pl.pallas_call(
        _ffn_kernel,
        grid=(m // tm,),
        in_specs=[row, _const_spec((1, d)), _const_spec(w_up_bf.shape), _const_spec(w_down_bf.shape),
                  _const_spec((1, d))],
        out_specs=row,
        out_shape=jax.ShapeDtypeStruct((m, d), F32),
        compiler_params=pltpu.CompilerParams(dimension_semantics=("arbitrary",), vmem_limit_bytes=VMEM_LIMIT),
        name="ffn",
    )(x2d, g_ffn.reshape(1, d), w_up_bf, w_down_bf, g_final.reshape(1, d))


def _rope_tables(pos):
    half = HEAD_DIM // 2
    inv = ROPE_THETA ** (-2.0 * jnp.arange(half, dtype=F32) / HEAD_DIM)
    ang = pos.astype(F32)[:, None] * inv[None, :]
    cos = jnp.tile(jnp.cos(ang), (1, 2 * N_HEADS))
    sin = jnp.tile(jnp.concatenate([-jnp.sin(ang), jnp.sin(ang)], axis=1), (1, N_HEADS))
    return cos, sin


def kernel(x_prompt, x_sample, cache_k, cache_v, page_table, g_attn, w_in, w_pa, w_pb, w_o, g_v, w_s, b_s,
           g_ffn, w_up, w_down, g_final):
    depth = w_in.shape[0]
    assert depth == 1, "the final norm is fused into the last layer's MLP kernel"
    bsz, seq, d = x_prompt.shape
    n_dec, dec_seq, _ = x_sample.shape
    assert dec_seq == 1 and seq % MOBA_BLOCK == 0
    past_len = page_table.shape[1] * cache_k.shape[2]
    n_chunk_rows = seq - ((seq - 1) // SGU_CHUNK) * SGU_CHUNK
    assert n_chunk_rows == SGU_CHUNK

    cos_p, sin_p = _rope_tables(jnp.arange(seq, dtype=jnp.int32))
    cos_s, sin_s = _rope_tables(jnp.full((n_dec,), past_len, dtype=jnp.int32))

    xp = x_prompt.reshape(bsz * seq, d)
    xs = x_sample.reshape(n_dec, d)
    l = 0
    w_in_bf, w_pa_bf, w_pb_bf, w_o_bf = (w[l].astype(BF16) for w in (w_in, w_pa, w_pb, w_o))
    w_up_bf, w_down_bf = w_up[l].astype(BF16), w_down[l].astype(BF16)

    tm_p = MOBA_BLOCK
    (q, k_p, v_p, kb, vb, u, vv, sga, sgb, sguv_p, kmean) = _inproj(
        xp, g_attn[l], w_in_bf, cos_p, sin_p, g_v[l], tm=tm_p, seq_tiles=seq // tm_p, channel_major_q=False,
        q_scale=PROMPT_Q_SCALE, emit_kmean=True)
    shp = (bsz, seq, ATTN_W)
    o_a = _prompt_attention(q.reshape(shp), kb.reshape(shp), vb.reshape(shp),
                            kmean.reshape(bsz, seq // MOBA_BLOCK, ATTN_W)).reshape(bsz * seq, ATTN_W)
    x1 = _merge(xp, o_a, u, vv, sga, sgb, w_s[l], b_s[l], w_pa_bf, w_pb_bf, w_o_bf, tm=tm_p, single_position=False)
    y_p = _ffn(x1, g_ffn[l], w_up_bf, w_down_bf, g_final, tm=512)

    (q_s, k_s, v_s, _, _, u_s, vv_s, sga_s, sgb_s, sguv_s) = _inproj(
        xs, g_attn[l], w_in_bf, cos_s, sin_s, g_v[l], tm=n_dec, seq_tiles=1, channel_major_q=True, q_scale=SCALE,
        emit_kmean=False)
    o_as = _sample_attention(q_s[0], k_s[0], v_s[0], cache_k[l], cache_v[l], page_table).T.astype(BF16)
    x1_s = _merge(xs, o_as, u_s, vv_s, sga_s, sgb_s, w_s[l], b_s[l], w_pa_bf, w_pb_bf, w_o_bf,
                  tm=n_dec, single_position=True)
    y_s = _ffn(x1_s, g_ffn[l], w_up_bf, w_down_bf, g_final, tm=n_dec)

    def heads_last(t, n, length):
        return t.reshape(1, n, N_HEADS, HEAD_DIM, length).transpose(0, 1, 4, 2, 3)

    return (y_p.reshape(bsz, seq, d),
            y_s.reshape(n_dec, 1, d),
            heads_last(k_p, bsz, seq),
            heads_last(v_p, bsz, seq),
            sguv_p.reshape(1, bsz, SGU_CHUNK, SGU_W),
            heads_last(k_s, 1, n_dec).reshape(1, n_dec, 1, N_HEADS, HEAD_DIM),
            heads_last(v_s, 1, n_dec).reshape(1, n_dec, 1, N_HEADS, HEAD_DIM),
            sguv_s.reshape(1, n_dec, 1, SGU_W))
```

```python
import functools

import jax
import jax.numpy as jnp
from jax import lax
from jax.experimental import pallas as pl
from jax.experimental.pallas import tpu as pltpu

N_HEADS = 8
HEAD_DIM = 64
ATTN_W = N_HEADS * HEAD_DIM
MOBA_BLOCK = 256
MOBA_TOPK = 3
SGU_GROUPS = 8
SGU_W = 512
SGU_GROUP_W = SGU_W // SGU_GROUPS
SGU_CHUNK = 128
ROPE_THETA = 10000.0
EPS = 1e-6
NEG = -1e30
SCALE = HEAD_DIM ** -0.5
LOG2E = 1.4426950408889634
PROMPT_Q_SCALE = SCALE * LOG2E

LANES = 128
F32_SUBLANES = 8
BF16_SUBLANES = 16
HEADS_PER_LANE_BLOCK = LANES // HEAD_DIM
VMEM_LIMIT = 56 * 1024 * 1024

F32 = jnp.float32
BF16 = jnp.bfloat16
NT_DIMS = (((1,), (1,)), ((), ()))


def _rmsnorm(x, g):
    return x * lax.rsqrt(jnp.mean(x * x, axis=-1, keepdims=True) + EPS) * g


def _const_spec(shape):
    return pl.BlockSpec(shape, lambda *_: (0,) * len(shape), pipeline_mode=pl.Buffered(1))


def _rope(t, cos, sin_signed):
    lane = lax.broadcasted_iota(jnp.int32, (t.shape[0], LANES), 1)
    first_half = (lane & (HEAD_DIM - 1)) < HEAD_DIM // 2
    outs = []
    for c in range(t.shape[1] // LANES):
        sl = slice(c * LANES, (c + 1) * LANES)
        tc = t[:, sl]
        partner = jnp.where(first_half, pltpu.roll(tc, LANES - HEAD_DIM // 2, 1),
                            pltpu.roll(tc, HEAD_DIM // 2, 1))
        outs.append(tc * cos[:, sl] + partner * sin_signed[:, sl])
    return jnp.concatenate(outs, axis=1)


def _inproj_kernel(x_ref, g_ref, w_ref, cos_ref, sin_ref, gv_ref,
                   q_ref, kf_ref, vf_ref, kb_ref, vb_ref, u_ref, vv_ref, sga_ref, sgb_ref, sguv_ref,
                   *maybe_kmean_and_scratch, q_scale):
    *maybe_kmean_ref, v_sc = maybe_kmean_and_scratch
    d_model = x_ref.shape[1]
    h = _rmsnorm(x_ref[...], g_ref[...]).astype(BF16)

    def proj(lo, width):
        return jnp.dot(h, w_ref[:, lo:lo + width], preferred_element_type=F32)

    cos = cos_ref[...]
    sin = sin_ref[...]
    q = _rope(proj(0, ATTN_W), cos, sin) * q_scale
    q_ref[0] = q.T.astype(q_ref.dtype)
    k = _rope(proj(ATTN_W, ATTN_W), cos, sin)
    kf_ref[0] = k.T
    kb_ref[...] = k.astype(BF16)
    if maybe_kmean_ref:
        maybe_kmean_ref[0][0] = jnp.mean(k, axis=0, keepdims=True)
    v_sc[...] = proj(2 * ATTN_W, ATTN_W)
    v_t = v_sc[...].T
    vf_ref[0] = v_t
    vb_ref[0] = v_t.astype(BF16)
    base = 3 * ATTN_W
    u_ref[...] = jax.nn.gelu(proj(base, SGU_W)).astype(BF16)
    vv = _rmsnorm(jax.nn.gelu(proj(base + SGU_W, SGU_W)), gv_ref[...])
    vv_ref[...] = vv.astype(BF16)
    sguv_ref[...] = vv[vv.shape[0] - sguv_ref.shape[0]:, :]
    base += 2 * SGU_W
    sga_ref[...] = jax.nn.sigmoid(proj(base, d_model)).astype(BF16)
    sgb_ref[...] = jax.nn.sigmoid(proj(base + d_model, d_model)).astype(BF16)


def _inproj(x2d, g_attn, w_in_bf, cos_tab, sin_tab, g_v, *, tm, seq_tiles, q_dtype, q_scale, emit_kmean):
    m, d = x2d.shape
    n_tiles = m // tm
    n_seq = n_tiles // seq_tiles
    keep = min(SGU_CHUNK, tm)
    row = lambda w: pl.BlockSpec((tm, w), lambda r: (r, 0))
    tab = pl.BlockSpec((tm, ATTN_W), lambda r: (r % seq_tiles, 0))
    chan = pl.BlockSpec((1, ATTN_W, tm), lambda r: (r // seq_tiles, 0, r % seq_tiles))
    chan_shape = lambda dt: jax.ShapeDtypeStruct((n_seq, ATTN_W, seq_tiles * tm), dt)
    out_shape = [
        chan_shape(q_dtype),
        chan_shape(F32),
        chan_shape(F32),
        jax.ShapeDtypeStruct((m, ATTN_W), BF16),
        chan_shape(BF16),
        jax.ShapeDtypeStruct((m, SGU_W), BF16),
        jax.ShapeDtypeStruct((m, SGU_W), BF16),
        jax.ShapeDtypeStruct((m, d), BF16),
        jax.ShapeDtypeStruct((m, d), BF16),
        jax.ShapeDtypeStruct((n_seq * keep, SGU_W), F32),
    ]
    out_specs = [chan, chan, chan, row(ATTN_W), chan] + [row(SGU_W)] * 2 + [row(d)] * 2 + [
        pl.BlockSpec((keep, SGU_W), lambda r: (r // seq_tiles, 0))]
    if emit_kmean:
        out_shape.append(jax.ShapeDtypeStruct((n_tiles, 1, ATTN_W), F32))
        out_specs.append(pl.BlockSpec((1, 1, ATTN_W), lambda r: (r, 0, 0)))
    return pl.pallas_call(
        functools.partial(_inproj_kernel, q_scale=q_scale),
        grid=(n_tiles,),
        in_specs=[row(d), _const_spec((1, d)), _const_spec(w_in_bf.shape), tab, tab, _const_spec((1, SGU_W))],
        out_specs=out_specs,
        out_shape=out_shape,
        scratch_shapes=[pltpu.VMEM((tm, ATTN_W), F32)],
        compiler_params=pltpu.CompilerParams(dimension_semantics=("arbitrary",), vmem_limit_bytes=VMEM_LIMIT),
        name="inproj",
    )(x2d, g_attn.reshape(1, d), w_in_bf, cos_tab, sin_tab, g_v.reshape(1, SGU_W))


def _moba_bias(gate, tile, n_blocks):
    blk = lax.broadcasted_iota(jnp.int32, gate.shape, 0).astype(F32)
    neg = NEG * PROMPT_Q_SCALE
    gate = jnp.where(blk < tile, gate, jnp.where(blk < n_blocks, neg, -jnp.inf))
    if tile <= MOBA_TOPK:
        return jnp.where(gate >= neg, 0.0, jnp.where(blk < tile, NEG, 0.0))
    cand = gate
    for t in range(MOBA_TOPK):
        best = jnp.max(cand, axis=0, keepdims=True)
        idx = jnp.min(jnp.where(cand == best, blk, float(gate.shape[0])), axis=0, keepdims=True)
        if t + 1 < MOBA_TOPK:
            cand = jnp.where(blk == idx, -jnp.inf, cand)
    bias = jnp.where(gate > best, 0.0, jnp.where(gate == best, jnp.where(blk <= idx, 0.0, NEG), NEG))
    return jnp.where(blk >= tile, 0.0, bias)


def _attn_tile(q_ref, k_ref, v_ref, km_ref, e_ref, o_ref, s_sc, tile):
    tq = MOBA_BLOCK
    g_n = HEADS_PER_LANE_BLOCK
    cols = g_n * tq
    n_blocks = km_ref.shape[1]
    q_t = q_ref[0]
    chan = lax.broadcasted_iota(jnp.int32, (LANES, tq), 0)
    qs = jnp.concatenate([jnp.where(chan // HEAD_DIM == g, q_t, jnp.zeros_like(q_t)) for g in range(g_n)], axis=1)

    if tile == 0:
        rhs = qs
    else:
        pad_rows = -n_blocks % BF16_SUBLANES
        km = jnp.concatenate([km_ref[0], jnp.zeros((pad_rows, LANES), F32)], axis=0)
        km_hi = km.astype(BF16)
        km_lo = (km - km_hi.astype(F32)).astype(BF16)
        gate = (jnp.dot(km_hi, qs, preferred_element_type=F32) + jnp.dot(km_lo, qs, preferred_element_type=F32))
        bias = _moba_bias(gate, tile, n_blocks).astype(BF16)
        rhs = jnp.concatenate([qs, bias, jnp.zeros((LANES - bias.shape[0], cols), BF16)], axis=0)

    kpos = lax.broadcasted_iota(jnp.int32, (MOBA_BLOCK, cols), 0)
    qpos = lax.broadcasted_iota(jnp.int32, (MOBA_BLOCK, cols), 1) & (tq - 1)
    groups = MOBA_BLOCK // F32_SUBLANES
    run_max = None
    for j in range(tile + 1):
        keys = k_ref[0, j * MOBA_BLOCK:(j + 1) * MOBA_BLOCK, :]
        if tile > 0:
            keys = jnp.concatenate([keys, e_ref[j * MOBA_BLOCK:(j + 1) * MOBA_BLOCK, :]], axis=1)
        s = jnp.dot(keys, rhs, preferred_element_type=F32)
        if j == tile:
            s = jnp.where(kpos <= qpos, s, NEG)
        s_sc[j] = s
        part = jnp.max(s.reshape(groups, F32_SUBLANES, cols), axis=0)
        run_max = part if run_max is None else jnp.maximum(run_max, part)
    m = jnp.max(run_max, axis=0, keepdims=True)

    acc = None
    run_sum = None
    for j in range(tile + 1):
        p = jnp.exp2(s_sc[j] - m)
        part = jnp.sum(p.reshape(groups, F32_SUBLANES, cols), axis=0)
        run_sum = part if run_sum is None else run_sum + part
        pv = jnp.dot(v_ref[0, :, j * MOBA_BLOCK:(j + 1) * MOBA_BLOCK], p.astype(BF16), preferred_element_type=F32)
        acc = pv if acc is None else acc + pv
    out = acc / jnp.sum(run_sum, axis=0, keepdims=True)
    o_t = jnp.zeros((LANES, tq), F32)
    for g in range(g_n):
        o_t = jnp.where(chan // HEAD_DIM == g, out[:, g * tq:(g + 1) * tq], o_t)
    o_ref[0] = o_t.T.astype(o_ref.dtype)


def _prompt_attn_kernel(q_ref, k_ref, v_ref, km_ref, e_ref, o_ref, s_sc, *, n_blocks):
    i = pl.program_id(2)
    for tile in range(n_blocks):
        @pl.when(i == tile)
        def _(tile=tile):
            _attn_tile(q_ref, k_ref, v_ref, km_ref, e_ref, o_ref, s_sc, tile)


def _prompt_attention(q_t, k, v_t, kmean):
    b, l, _ = k.shape
    n_blocks = l // MOBA_BLOCK
    assert n_blocks <= LANES
    cols = HEADS_PER_LANE_BLOCK * MOBA_BLOCK
    block_of_key = (jnp.arange(l, dtype=jnp.int32) // MOBA_BLOCK)[:, None]
    onehot = (block_of_key == jnp.arange(LANES, dtype=jnp.int32)[None, :]).astype(BF16)
    return pl.pallas_call(
        functools.partial(_prompt_attn_kernel, n_blocks=n_blocks),
        grid=(b, ATTN_W // LANES, n_blocks),
        in_specs=[
            pl.BlockSpec((1, LANES, MOBA_BLOCK), lambda bi, c, i: (bi, c, i)),
            pl.BlockSpec((1, l, LANES), lambda bi, c, i: (bi, 0, c)),
            pl.BlockSpec((1, LANES, l), lambda bi, c, i: (bi, c, 0)),
            pl.BlockSpec((1, n_blocks, LANES), lambda bi, c, i: (bi, 0, c)),
            _const_spec((l, LANES)),
        ],
        out_specs=pl.BlockSpec((1, MOBA_BLOCK, LANES), lambda bi, c, i: (bi, i, c)),
        out_shape=jax.ShapeDtypeStruct((b, l, ATTN_W), BF16),
        scratch_shapes=[pltpu.VMEM((n_blocks, MOBA_BLOCK, cols), F32)],
        compiler_params=pltpu.CompilerParams(
            dimension_semantics=("arbitrary", "arbitrary", "arbitrary"), vmem_limit_bytes=VMEM_LIMIT),
        name="prompt_attn",
    )(q_t, k, v_t, kmean, onehot)


def _sample_attn_kernel(pt_ref, q_ref, kn_ref, vn_ref, ck_hbm, cv_hbm, o_ref,
                        kbuf, vbuf, qb_sc, s_sc, sem_k, sem_v, *, n_pages, page):
    b = pl.program_id(0)
    slot = b % 2
    ppb = MOBA_BLOCK // page
    n_past = n_pages // ppb

    def k_copy(seq, sl, p):
        return pltpu.make_async_copy(ck_hbm.at[pt_ref[seq, p]], kbuf.at[sl, p], sem_k.at[sl])

    @pl.when(b == 0)
    def _():
        o_ref[...] = jnp.zeros_like(o_ref)
        for p in range(n_pages):
            k_copy(0, 0, p).start()

    for p in range(n_pages):
        k_copy(b, slot, p).wait()

    @pl.when(b + 1 < pl.num_programs(0))
    def _():
        for p in range(n_pages):
            k_copy(b + 1, 1 - slot, p).start()

    def column(ref):
        a = ref[...]
        seq_id = lax.broadcasted_iota(jnp.int32, a.shape, 1)
        return jnp.sum(jnp.where(seq_id == b, a, 0.0), axis=1, keepdims=True)

    q_col, kn_col, vn_col = column(q_ref), column(kn_ref), column(vn_ref)
    qb_sc[...] = jnp.broadcast_to(q_col, qb_sc.shape)

    def page_scores(kt):
        return jnp.sum((kt * qb_sc[...]).reshape(N_HEADS, HEAD_DIM, page), axis=1)

    def score_body(p, carry):
        s_sc[p] = page_scores(kbuf[slot, p])
        return carry
    lax.fori_loop(0, n_pages, score_body, 0, unroll=2)

    pos_w = lax.broadcasted_iota(jnp.int32, (ATTN_W, page), 1)
    s_self = page_scores(jnp.where(pos_w == 0, kn_col, 0.0))

    hl = lax.broadcasted_iota(jnp.int32, (N_HEADS, LANES), 1)
    cand = jnp.where(hl == n_past, NEG * SCALE, -jnp.inf).astype(F32)
    for j in range(n_past):
        blk = s_sc[j * ppb]
        for r in range(1, ppb):
            blk = blk + s_sc[j * ppb + r]
        cand = jnp.where(hl == j, jnp.sum(blk, axis=1, keepdims=True) * (1.0 / MOBA_BLOCK), cand)
    rank = jnp.zeros((N_HEADS, LANES), F32)
    for j in range(n_past + 1):
        gj = cand[:, j:j + 1]
        rank = rank + jnp.where(gj > cand, 1.0, jnp.where(gj == cand, jnp.where(hl > j, 1.0, 0.0), 0.0))
    hl_f = hl.astype(F32)
    picks = [jnp.sum(jnp.where(rank == t, hl_f, 0.0), axis=1, keepdims=True).astype(jnp.int32)
             for t in range(MOBA_TOPK)]

    blocks = [[picks[t][h, 0] for t in range(MOBA_TOPK)] for h in range(N_HEADS)]
    clamped = [[jnp.minimum(blk, n_past - 1) for blk in row] for row in blocks]

    def v_copy(h, t, r):
        phys = pt_ref[b, clamped[h][t] * ppb + r]
        return pltpu.make_async_copy(cv_hbm.at[phys, pl.ds(h * HEAD_DIM, HEAD_DIM), :],
                                     vbuf.at[(h * MOBA_TOPK + t) * ppb + r], sem_v)

    for h in range(N_HEADS):
        for t in range(MOBA_TOPK):
            for r in range(ppb):
                v_copy(h, t, r).start(priority=1)

    s_all = s_sc[...]
    blk_id = lax.broadcasted_iota(jnp.int32, s_all.shape, 0) // ppb
    chosen = blk_id == picks[0][None]
    for t in range(1, MOBA_TOPK):
        chosen = chosen | (blk_id == picks[t][None])
    s_all = jnp.where(chosen, s_all, NEG)
    s_self = jnp.where(hl == 0, s_self, NEG)
    m = jnp.max(jnp.maximum(jnp.max(s_all, axis=0), s_self), axis=1, keepdims=True)
    p_all = jnp.exp(s_all - m[None])
    p_self = jnp.exp(s_self - m)
    denom = jnp.sum(jnp.sum(p_all, axis=0) + p_self, axis=1, keepdims=True)
    s_sc[...] = p_all / denom[None]
    p_self = p_self / denom

    for h in range(N_HEADS):
        for t in range(MOBA_TOPK):
            for r in range(ppb):
                v_copy(h, t, r).wait()

    pos_h = lax.broadcasted_iota(jnp.int32, (HEAD_DIM, page), 1)
    cols = []
    for h in range(N_HEADS):
        rows = slice(h * HEAD_DIM, (h + 1) * HEAD_DIM)
        acc = p_self[h:h + 1, :] * jnp.where(pos_h == 0, vn_col[rows], 0.0)
        for t in range(MOBA_TOPK):
            for r in range(ppb):
                p_row = s_sc[clamped[h][t] * ppb + r, pl.ds(h, 1), :]
                p_row = jnp.where(blocks[h][t] < n_past, p_row, 0.0)
                acc = acc + p_row * vbuf[(h * MOBA_TOPK + t) * ppb + r]
        cols.append(jnp.sum(acc, axis=1, keepdims=True))
    o_col = jnp.concatenate(cols, axis=0)
    seq_id = lax.broadcasted_iota(jnp.int32, o_ref.shape, 1)
    o_ref[...] = jnp.where(seq_id == b, o_col, o_ref[...])


def _sample_attention(q_t, k_new_t, v_new_t, cache_k, cache_v, page_table):
    s, n_pages = page_table.shape
    n_phys, page = cache_k.shape[0], cache_k.shape[1]
    assert page == LANES and MOBA_BLOCK % page == 0 and (n_pages * page) % MOBA_BLOCK == 0
    n_past = n_pages * page // MOBA_BLOCK
    assert 1 <= n_past < LANES and s <= LANES
    ck = cache_k.transpose(0, 2, 3, 1).reshape(n_phys, ATTN_W, page)
    cv = cache_v.transpose(0, 2, 3, 1).reshape(n_phys, ATTN_W, page)
    n_vbuf = N_HEADS * MOBA_TOPK * (MOBA_BLOCK // page)
    tok = pl.BlockSpec((ATTN_W, s), lambda b, pt: (0, 0))
    grid_spec = pltpu.PrefetchScalarGridSpec(
        num_scalar_prefetch=1,
        grid=(s,),
        in_specs=[tok, tok, tok, pl.BlockSpec(memory_space=pl.ANY), pl.BlockSpec(memory_space=pl.ANY)],
        out_specs=tok,
        scratch_shapes=[
            pltpu.VMEM((2, n_pages, ATTN_W, page), F32),
            pltpu.VMEM((n_vbuf, HEAD_DIM, page), F32),
            pltpu.VMEM((ATTN_W, page), F32),
            pltpu.VMEM((n_pages, N_HEADS, page), F32),
            pltpu.SemaphoreType.DMA((2,)),
            pltpu.SemaphoreType.DMA(()),
        ],
    )
    return pl.pallas_call(
        functools.partial(_sample_attn_kernel, n_pages=n_pages, page=page),
        grid_spec=grid_spec,
        out_shape=jax.ShapeDtypeStruct((ATTN_W, s), F32),
        compiler_params=pltpu.CompilerParams(dimension_semantics=("arbitrary",), vmem_limit_bytes=VMEM_LIMIT),
        name="sample_attn",
    )(page_table, q_t, k_new_t, v_new_t, ck, cv)


def _merge_kernel(x_ref, oa_ref, u_ref, vv_ref, sga_ref, sgb_ref, ws_ref, bs_ref, wpa_ref, wpb_ref, wo_ref,
                  x1_ref, *, single_position):
    tm = x_ref.shape[0]
    if single_position:
        s = vv_ref[...].astype(F32) * ws_ref[...] + bs_ref[...]
    else:
        t_idx = lax.broadcasted_iota(jnp.int32, (SGU_CHUNK, SGU_CHUNK), 0)
        s_idx = lax.broadcasted_iota(jnp.int32, (SGU_CHUNK, SGU_CHUNK), 1)
        ws = [jnp.where(s_idx <= t_idx, ws_ref[g], 0.0).astype(BF16) for g in range(SGU_GROUPS)]
        lane = lax.broadcasted_iota(jnp.int32, (SGU_CHUNK, LANES), 1)
        groups_per_block = LANES // SGU_GROUP_W
        chunks = []
        for c in range(tm // SGU_CHUNK):
            parts = []
            for blk in range(SGU_W // LANES):
                vp = vv_ref[c * SGU_CHUNK:(c + 1) * SGU_CHUNK, blk * LANES:(blk + 1) * LANES]
                sp = jnp.zeros((SGU_CHUNK, LANES), F32)
                for g in range(groups_per_block):
                    sg = jnp.dot(ws[blk * groups_per_block + g], vp, preferred_element_type=F32)
                    sp = jnp.where(lane // SGU_GROUP_W == g, sg, sp)
                parts.append(sp)
            chunks.append(jnp.concatenate(parts, axis=1) + bs_ref[...])
        s = jnp.concatenate(chunks, axis=0)
    ob = (u_ref[...].astype(F32) * s).astype(BF16)
    ba = jnp.dot(oa_ref[...], wpa_ref[...], preferred_element_type=F32)
    bb = jnp.dot(ob, wpb_ref[...], preferred_element_type=F32)
    merged = (sga_ref[...].astype(F32) * ba + sgb_ref[...].astype(F32) * bb).astype(BF16)
    x1_ref[...] = x_ref[...] + jnp.dot(merged, wo_ref[...], preferred_element_type=F32)


def _merge(x2d, o_a, u, vv, sga, sgb, w_s, b_s, w_pa_bf, w_pb_bf, w_o_bf, *, tm, single_position):
    m, d = x2d.shape
    if single_position:
        ws_in = jnp.repeat(w_s[:, 0, 0], SGU_GROUP_W).reshape(1, SGU_W)
        bs_in = jnp.repeat(b_s[:, 0], SGU_GROUP_W).reshape(1, SGU_W)
    else:
        ws_in = w_s
        bs_in = jnp.repeat(b_s.T, SGU_GROUP_W, axis=1)
    row = lambda w: pl.BlockSpec((tm, w), lambda r: (r, 0))
    return pl.pallas_call(
        functools.partial(_merge_kernel, single_position=single_position),
        grid=(m // tm,),
        in_specs=[row(d), row(ATTN_W), row(SGU_W), row(SGU_W), row(d), row(d),
                  _const_spec(ws_in.shape), _const_spec(bs_in.shape),
                  _const_spec(w_pa_bf.shape), _const_spec(w_pb_bf.shape), _const_spec(w_o_bf.shape)],
        out_specs=row(d),
        out_shape=jax.ShapeDtypeStruct((m, d), F32),
        compiler_params=pltpu.CompilerParams(dimension_semantics=("arbitrary",), vmem_limit_bytes=VMEM_LIMIT),
        name="merge",
    )(x2d, o_a, u, vv, sga, sgb, ws_in, bs_in, w_pa_bf, w_pb_bf, w_o_bf)


FFN_CHUNK = 1024


def _ffn_kernel(x_ref, g_ref, wup_ref, wdn_ref, gf_ref, y_ref):
    x = x_ref[...]
    h = _rmsnorm(x, g_ref[...]).astype(BF16)
    acc = x
    for c in range(wup_ref.shape[1] // FFN_CHUNK):
        a = jnp.dot(h, wup_ref[:, c * FFN_CHUNK:(c + 1) * FFN_CHUNK], preferred_element_type=F32)
        a = jnp.square(jnp.maximum(a, 0.0)).astype(BF16)
        acc = acc + jnp.dot(a, wdn_ref[c * FFN_CHUNK:(c + 1) * FFN_CHUNK, :], preferred_element_type=F32)
    y_ref[...] = _rmsnorm(acc, gf_ref[...])


def _ffn(x2d, g_ffn, w_up_bf, w_down_bf, g_final, *, tm):
    m, d = x2d.shape
    row = pl.BlockSpec((tm, d), lambda r: (r, 0))
    return pl.pallas_call(
        _ffn_kernel,
        grid=(m // tm,),
        in_specs=[row, _const_spec((1, d)), _const_spec(w_up_bf.shape), _const_spec(w_down_bf.shape),
                  _const_spec((1, d))],
        out_specs=row,
        out_shape=jax.ShapeDtypeStruct((m, d), F32),
        compiler_params=pltpu.CompilerParams(dimension_semantics=("arbitrary",), vmem_limit_bytes=VMEM_LIMIT),
        name="ffn",
    )(x2d, g_ffn.reshape(1, d), w_up_bf, w_down_bf, g_final.reshape(1, d))


def _rope_tables(pos):
    half = HEAD_DIM // 2
    inv = ROPE_THETA ** (-2.0 * jnp.arange(half, dtype=F32) / HEAD_DIM)
    ang = pos.astype(F32)[:, None] * inv[None, :]
    cos = jnp.tile(jnp.cos(ang), (1, 2 * N_HEADS))
    sin = jnp.tile(jnp.concatenate([-jnp.sin(ang), jnp.sin(ang)], axis=1), (1, N_HEADS))
    return cos, sin


def kernel(x_prompt, x_sample, cache_k, cache_v, page_table, g_attn, w_in, w_pa, w_pb, w_o, g_v, w_s, b_s,
           g_ffn, w_up, w_down, g_final):
    depth = w_in.shape[0]
    assert depth == 1, "the final norm is fused into the last layer's MLP kernel"
    bsz, seq, d = x_prompt.shape
    n_dec, dec_seq, _ = x_sample.shape
    assert dec_seq == 1 and seq % MOBA_BLOCK == 0
    past_len = page_table.shape[1] * cache_k.shape[2]
    n_chunk_rows = seq - ((seq - 1) // SGU_CHUNK) * SGU_CHUNK
    assert n_chunk_rows == SGU_CHUNK

    cos_p, sin_p = _rope_tables(jnp.arange(seq, dtype=jnp.int32))
    cos_s, sin_s = _rope_tables(jnp.full((n_dec,), past_len, dtype=jnp.int32))

    xp = x_prompt.reshape(bsz * seq, d)
    xs = x_sample.reshape(n_dec, d)
    l = 0
    w_in_bf, w_pa_bf, w_pb_bf, w_o_bf = (w[l].astype(BF16) for w in (w_in, w_pa, w_pb, w_o))
    w_up_bf, w_down_bf = w_up[l].astype(BF16), w_down[l].astype(BF16)

    tm_p = MOBA_BLOCK
    (q_t, k_p, v_p, kb, vb_t, u, vv, sga, sgb, sguv_p, kmean) = _inproj(
        xp, g_attn[l], w_in_bf, cos_p, sin_p, g_v[l], tm=tm_p, seq_tiles=seq // tm_p, q_dtype=BF16,
        q_scale=PROMPT_Q_SCALE, emit_kmean=True)
    o_a = _prompt_attention(q_t, kb.reshape(bsz, seq, ATTN_W), vb_t,
                            kmean.reshape(bsz, seq // MOBA_BLOCK, ATTN_W)).reshape(bsz * seq, ATTN_W)
    x1 = _merge(xp, o_a, u, vv, sga, sgb, w_s[l], b_s[l], w_pa_bf, w_pb_bf, w_o_bf, tm=tm_p, single_position=False)
    y_p = _ffn(x1, g_ffn[l], w_up_bf, w_down_bf, g_final, tm=512)

    (q_s, k_s, v_s, _, _, u_s, vv_s, sga_s, sgb_s, sguv_s) = _inproj(
        xs, g_attn[l], w_in_bf, cos_s, sin_s, g_v[l], tm=n_dec, seq_tiles=1, q_dtype=F32, q_scale=SCALE,
        emit_kmean=False)
    o_as = _sample_attention(q_s[0], k_s[0], v_s[0], cache_k[l], cache_v[l], page_table).T.astype(BF16)
    x1_s = _merge(xs, o_as, u_s, vv_s, sga_s, sgb_s, w_s[l], b_s[l], w_pa_bf, w_pb_bf, w_o_bf,
                  tm=n_dec, single_position=True)
    y_s = _ffn(x1_s, g_ffn[l], w_up_bf, w_down_bf, g_final, tm=n_dec)

    def heads_last(t, n, length):
        return t.reshape(1, n, N_HEADS, HEAD_DIM, length).transpose(0, 1, 4, 2, 3)

    return (y_p.reshape(bsz, seq, d),
            y_s.reshape(n_dec, 1, d),
            heads_last(k_p, bsz, seq),
            heads_last(v_p, bsz, seq),
            sguv_p.reshape(1, bsz, SGU_CHUNK, SGU_W),
            heads_last(k_s, 1, n_dec).reshape(1, n_dec, 1, N_HEADS, HEAD_DIM),
            heads_last(v_s, 1, n_dec).reshape(1, n_dec, 1, N_HEADS, HEAD_DIM),
            sguv_s.reshape(1, n_dec, 1, SGU_W))
```

```python
import functools

import jax
import jax.numpy as jnp
from jax import lax
from jax.experimental import pallas as pl
from jax.experimental.pallas import tpu as pltpu

N_HEADS = 8
HEAD_DIM = 64
ATTN_W = N_HEADS * HEAD_DIM
MOBA_BLOCK = 256
MOBA_TOPK = 3
SGU_GROUPS = 8
SGU_W = 512
SGU_GROUP_W = SGU_W // SGU_GROUPS
SGU_CHUNK = 128
ROPE_THETA = 10000.0
EPS = 1e-6
NEG = -1e30
SCALE = HEAD_DIM ** -0.5
LOG2E = 1.4426950408889634
PROMPT_Q_SCALE = SCALE * LOG2E

LANES = 128
F32_SUBLANES = 8
BF16_SUBLANES = 16
HEADS_PER_LANE_BLOCK = LANES // HEAD_DIM
VMEM_LIMIT = 56 * 1024 * 1024
SAMPLE_CHUNK_PAGES = 16
SAMPLE_SEQS_IN_FLIGHT = 2

F32 = jnp.float32
BF16 = jnp.bfloat16
NT_DIMS = (((1,), (1,)), ((), ()))


def _rmsnorm(x, g):
    return x * lax.rsqrt(jnp.mean(x * x, axis=-1, keepdims=True) + EPS) * g


def _const_spec(shape):
    return pl.BlockSpec(shape, lambda *_: (0,) * len(shape), pipeline_mode=pl.Buffered(1))


def _rope(t, cos, sin_signed):
    lane = lax.broadcasted_iota(jnp.int32, (t.shape[0], LANES), 1)
    first_half = (lane & (HEAD_DIM - 1)) < HEAD_DIM // 2
    outs = []
    for c in range(t.shape[1] // LANES):
        sl = slice(c * LANES, (c + 1) * LANES)
        tc = t[:, sl]
        partner = jnp.where(first_half, pltpu.roll(tc, LANES - HEAD_DIM // 2, 1),
                            pltpu.roll(tc, HEAD_DIM // 2, 1))
        outs.append(tc * cos[:, sl] + partner * sin_signed[:, sl])
    return jnp.concatenate(outs, axis=1)


def _inproj_kernel(x_ref, g_ref, w_ref, cos_ref, sin_ref, gv_ref,
                   q_ref, kf_ref, vf_ref, kb_ref, vb_ref, u_ref, vv_ref, sga_ref, sgb_ref, sguv_ref,
                   *maybe_kmean_and_scratch, q_scale):
    *maybe_kmean_ref, v_sc = maybe_kmean_and_scratch
    d_model = x_ref.shape[1]
    h = _rmsnorm(x_ref[...], g_ref[...]).astype(BF16)

    def proj(lo, width):
        return jnp.dot(h, w_ref[:, lo:lo + width], preferred_element_type=F32)

    cos = cos_ref[...]
    sin = sin_ref[...]
    q = _rope(proj(0, ATTN_W), cos, sin) * q_scale
    q_ref[0] = q.T.astype(q_ref.dtype)
    k = _rope(proj(ATTN_W, ATTN_W), cos, sin)
    kf_ref[0] = k.T
    kb_ref[...] = k.astype(BF16)
    if maybe_kmean_ref:
        maybe_kmean_ref[0][0] = jnp.mean(k, axis=0, keepdims=True)
    v_sc[...] = proj(2 * ATTN_W, ATTN_W)
    v_t = v_sc[...].T
    vf_ref[0] = v_t
    vb_ref[0] = v_t.astype(BF16)
    base = 3 * ATTN_W
    u_ref[...] = jax.nn.gelu(proj(base, SGU_W)).astype(BF16)
    vv = _rmsnorm(jax.nn.gelu(proj(base + SGU_W, SGU_W)), gv_ref[...])
    vv_ref[...] = vv.astype(BF16)
    sguv_ref[...] = vv[vv.shape[0] - sguv_ref.shape[0]:, :]
    base += 2 * SGU_W
    sga_ref[...] = jax.nn.sigmoid(proj(base, d_model)).astype(BF16)
    sgb_ref[...] = jax.nn.sigmoid(proj(base + d_model, d_model)).astype(BF16)


def _inproj(x2d, g_attn, w_in_bf, cos_tab, sin_tab, g_v, *, tm, seq_tiles, q_dtype, q_scale, emit_kmean):
    m, d = x2d.shape
    n_tiles = m // tm
    n_seq = n_tiles // seq_tiles
    keep = min(SGU_CHUNK, tm)
    row = lambda w: pl.BlockSpec((tm, w), lambda r: (r, 0))
    tab = pl.BlockSpec((tm, ATTN_W), lambda r: (r % seq_tiles, 0))
    chan = pl.BlockSpec((1, ATTN_W, tm), lambda r: (r // seq_tiles, 0, r % seq_tiles))
    chan_shape = lambda dt: jax.ShapeDtypeStruct((n_seq, ATTN_W, seq_tiles * tm), dt)
    out_shape = [
        chan_shape(q_dtype),
        chan_shape(F32),
        chan_shape(F32),
        jax.ShapeDtypeStruct((m, ATTN_W), BF16),
        chan_shape(BF16),
        jax.ShapeDtypeStruct((m, SGU_W), BF16),
        jax.ShapeDtypeStruct((m, SGU_W), BF16),
        jax.ShapeDtypeStruct((m, d), BF16),
        jax.ShapeDtypeStruct((m, d), BF16),
        jax.ShapeDtypeStruct((n_seq * keep, SGU_W), F32),
    ]
    out_specs = [chan, chan, chan, row(ATTN_W), chan] + [row(SGU_W)] * 2 + [row(d)] * 2 + [
        pl.BlockSpec((keep, SGU_W), lambda r: (r // seq_tiles, 0))]
    if emit_kmean:
        out_shape.append(jax.ShapeDtypeStruct((n_tiles, 1, ATTN_W), F32))
        out_specs.append(pl.BlockSpec((1, 1, ATTN_W), lambda r: (r, 0, 0)))
    return pl.pallas_call(
        functools.partial(_inproj_kernel, q_scale=q_scale),
        grid=(n_tiles,),
        in_specs=[row(d), _const_spec((1, d)), _const_spec(w_in_bf.shape), tab, tab, _const_spec((1, SGU_W))],
        out_specs=out_specs,
        out_shape=out_shape,
        scratch_shapes=[pltpu.VMEM((tm, ATTN_W), F32)],
        compiler_params=pltpu.CompilerParams(dimension_semantics=("arbitrary",), vmem_limit_bytes=VMEM_LIMIT),
        name="inproj",
    )(x2d, g_attn.reshape(1, d), w_in_bf, cos_tab, sin_tab, g_v.reshape(1, SGU_W))


def _moba_bias(gate, tile, n_blocks):
    blk = lax.broadcasted_iota(jnp.int32, gate.shape, 0).astype(F32)
    neg = NEG * PROMPT_Q_SCALE
    gate = jnp.where(blk < tile, gate, jnp.where(blk < n_blocks, neg, -jnp.inf))
    if tile <= MOBA_TOPK:
        return jnp.where(gate >= neg, 0.0, jnp.where(blk < tile, NEG, 0.0))
    cand = gate
    for t in range(MOBA_TOPK):
        best = jnp.max(cand, axis=0, keepdims=True)
        idx = jnp.min(jnp.where(cand == best, blk, float(gate.shape[0])), axis=0, keepdims=True)
        if t + 1 < MOBA_TOPK:
            cand = jnp.where(blk == idx, -jnp.inf, cand)
    bias = jnp.where(gate > best, 0.0, jnp.where(gate == best, jnp.where(blk <= idx, 0.0, NEG), NEG))
    return jnp.where(blk >= tile, 0.0, bias)


def _attn_tile(q_ref, k_ref, v_ref, km_ref, o_ref, s_sc, tile):
    tq = MOBA_BLOCK
    g_n = HEADS_PER_LANE_BLOCK
    cols = g_n * tq
    n_blocks = km_ref.shape[1]
    q_t = q_ref[0]
    chan = lax.broadcasted_iota(jnp.int32, (LANES, tq), 0)
    qs = jnp.concatenate([jnp.where(chan // HEAD_DIM == g, q_t, jnp.zeros_like(q_t)) for g in range(g_n)], axis=1)

    bias = None
    if tile > 0:
        pad_rows = -n_blocks % BF16_SUBLANES
        km = jnp.concatenate([km_ref[0], jnp.zeros((pad_rows, LANES), F32)], axis=0)
        km_hi = km.astype(BF16)
        km_lo = (km - km_hi.astype(F32)).astype(BF16)
        gate = (jnp.dot(km_hi, qs, preferred_element_type=F32) + jnp.dot(km_lo, qs, preferred_element_type=F32))
        bias = _moba_bias(gate, tile, n_blocks)

    s_all = jnp.dot(k_ref[0, 0:(tile + 1) * MOBA_BLOCK, :], qs, preferred_element_type=F32)
    kpos = lax.broadcasted_iota(jnp.int32, (MOBA_BLOCK, cols), 0)
    qpos = lax.broadcasted_iota(jnp.int32, (MOBA_BLOCK, cols), 1) & (tq - 1)
    groups = MOBA_BLOCK // F32_SUBLANES
    run_max = None
    for j in range(tile + 1):
        s = s_all[j * MOBA_BLOCK:(j + 1) * MOBA_BLOCK]
        if j == tile:
            s = jnp.where(kpos <= qpos, s, NEG)
        s_sc[j] = s
        part = jnp.max(s.reshape(groups, F32_SUBLANES, cols), axis=0)
        if j < tile:
            part = part + bias[j:j + 1, :]
        run_max = part if run_max is None else jnp.maximum(run_max, part)
    m = jnp.max(run_max, axis=0, keepdims=True)

    acc = None
    run_sum = None
    for j in range(tile + 1):
        shift = m - bias[j:j + 1, :] if j < tile else m
        p = jnp.exp2(s_sc[j] - shift)
        part = jnp.sum(p.reshape(groups, F32_SUBLANES, cols), axis=0)
        run_sum = part if run_sum is None else run_sum + part
        pv = jnp.dot(v_ref[0, :, j * MOBA_BLOCK:(j + 1) * MOBA_BLOCK], p.astype(BF16), preferred_element_type=F32)
        acc = pv if acc is None else acc + pv
    out = acc / jnp.sum(run_sum, axis=0, keepdims=True)
    o_t = jnp.zeros((LANES, tq), F32)
    for g in range(g_n):
        o_t = jnp.where(chan // HEAD_DIM == g, out[:, g * tq:(g + 1) * tq], o_t)
    o_ref[0] = o_t.T.astype(o_ref.dtype)


def _prompt_attn_kernel(q_ref, k_ref, v_ref, km_ref, o_ref, s_sc, *, n_blocks):
    i = pl.program_id(2)
    for tile in range(n_blocks):
        @pl.when(i == tile)
        def _(tile=tile):
            _attn_tile(q_ref, k_ref, v_ref, km_ref, o_ref, s_sc, tile)


def _prompt_attention(q_t, k, v_t, kmean):
    b, l, _ = k.shape
    n_blocks = l // MOBA_BLOCK
    assert n_blocks <= LANES
    cols = HEADS_PER_LANE_BLOCK * MOBA_BLOCK
    return pl.pallas_call(
        functools.partial(_prompt_attn_kernel, n_blocks=n_blocks),
        grid=(b, ATTN_W // LANES, n_blocks),
        in_specs=[
            pl.BlockSpec((1, LANES, MOBA_BLOCK), lambda bi, c, i: (bi, c, i)),
            pl.BlockSpec((1, l, LANES), lambda bi, c, i: (bi, 0, c)),
            pl.BlockSpec((1, LANES, l), lambda bi, c, i: (bi, c, 0)),
            pl.BlockSpec((1, n_blocks, LANES), lambda bi, c, i: (bi, 0, c)),
        ],
        out_specs=pl.BlockSpec((1, MOBA_BLOCK, LANES), lambda bi, c, i: (bi, i, c)),
        out_shape=jax.ShapeDtypeStruct((b, l, ATTN_W), BF16),
        scratch_shapes=[pltpu.VMEM((n_blocks, MOBA_BLOCK, cols), F32)],
        compiler_params=pltpu.CompilerParams(
            dimension_semantics=("arbitrary", "arbitrary", "arbitrary"), vmem_limit_bytes=VMEM_LIMIT),
        name="prompt_attn",
    )(q_t, k, v_t, kmean)


def _sample_attn_kernel(pt_ref, q_ref, kn_ref, vn_ref, ck_hbm, cv_hbm, o_ref,
                        kbuf, vbuf, qb_sc, s_sc, sem_k, sem_v, *, n_pages, page):
    b = pl.program_id(0)
    n_seq = pl.num_programs(0)
    ppb = MOBA_BLOCK // page
    n_past = n_pages // ppb
    chunk_pages = kbuf.shape[1]
    cps = n_pages // chunk_pages
    ahead = kbuf.shape[0] // cps

    def k_copy(seq, c, p):
        sl = (seq % ahead) * cps + c
        return pltpu.make_async_copy(ck_hbm.at[pt_ref[seq, c * chunk_pages + p]], kbuf.at[sl, p], sem_k.at[sl])

    @pl.when(b == 0)
    def _():
        o_ref[...] = jnp.zeros_like(o_ref)
        for seq in range(ahead):
            @pl.when(seq < n_seq)
            def _(seq=seq):
                for c in range(cps):
                    for p in range(chunk_pages):
                        k_copy(seq, c, p).start()

    def column(ref):
        a = ref[...]
        seq_id = lax.broadcasted_iota(jnp.int32, a.shape, 1)
        return jnp.sum(jnp.where(seq_id == b, a, 0.0), axis=1, keepdims=True)

    q_col, kn_col, vn_col = column(q_ref), column(kn_ref), column(vn_ref)
    qb_sc[...] = jnp.broadcast_to(q_col, qb_sc.shape)

    def page_scores(kt):
        return jnp.sum((kt * qb_sc[...]).reshape(N_HEADS, HEAD_DIM, page), axis=1)

    for c in range(cps):
        sl = (b % ahead) * cps + c
        for p in range(chunk_pages):
            k_copy(b, c, p).wait()

        def score_body(p, carry, c=c, sl=sl):
            s_sc[c * chunk_pages + p] = page_scores(kbuf[sl, p])
            return carry
        lax.fori_loop(0, chunk_pages, score_body, 0, unroll=2)

        @pl.when(b + ahead < n_seq)
        def _(c=c):
            for p in range(chunk_pages):
                k_copy(b + ahead, c, p).start()

    pos_w = lax.broadcasted_iota(jnp.int32, (ATTN_W, page), 1)
    s_self = page_scores(jnp.where(pos_w == 0, kn_col, 0.0))

    hl = lax.broadcasted_iota(jnp.int32, (N_HEADS, LANES), 1)
    cand = jnp.where(hl == n_past, NEG * SCALE, -jnp.inf).astype(F32)
    for j in range(n_past):
        blk = s_sc[j * ppb]
        for r in range(1, ppb):
            blk = blk + s_sc[j * ppb + r]
        cand = jnp.where(hl == j, jnp.sum(blk, axis=1, keepdims=True) * (1.0 / MOBA_BLOCK), cand)
    rank = jnp.zeros((N_HEADS, LANES), F32)
    for j in range(n_past + 1):
        gj = cand[:, j:j + 1]
        rank = rank + jnp.where(gj > cand, 1.0, jnp.where(gj == cand, jnp.where(hl > j, 1.0, 0.0), 0.0))
    hl_f = hl.astype(F32)
    picks = [jnp.sum(jnp.where(rank == t, hl_f, 0.0), axis=1, keepdims=True).astype(jnp.int32)
             for t in range(MOBA_TOPK)]

    blocks = [[picks[t][h, 0] for t in range(MOBA_TOPK)] for h in range(N_HEADS)]
    clamped = [[jnp.minimum(blk, n_past - 1) for blk in row] for row in blocks]

    def v_copy(h, t, r):
        phys = pt_ref[b, clamped[h][t] * ppb + r]
        return pltpu.make_async_copy(cv_hbm.at[phys, pl.ds(h * HEAD_DIM, HEAD_DIM), :],
                                     vbuf.at[(h * MOBA_TOPK + t) * ppb + r], sem_v)

    for h in range(N_HEADS):
        for t in range(MOBA_TOPK):
            for r in range(ppb):
                v_copy(h, t, r).start(priority=1)

    s_all = s_sc[...]
    blk_id = lax.broadcasted_iota(jnp.int32, s_all.shape, 0) // ppb
    chosen = blk_id == picks[0][None]
    for t in range(1, MOBA_TOPK):
        chosen = chosen | (blk_id == picks[t][None])
    s_all = jnp.where(chosen, s_all, NEG)
    s_self = jnp.where(hl == 0, s_self, NEG)
    m = jnp.max(jnp.maximum(jnp.max(s_all, axis=0), s_self), axis=1, keepdims=True)
    p_all = jnp.exp(s_all - m[None])
    p_self = jnp.exp(s_self - m)
    denom = jnp.sum(jnp.sum(p_all, axis=0) + p_self, axis=1, keepdims=True)
    s_sc[...] = p_all / denom[None]
    p_self = p_self / denom

    for h in range(N_HEADS):
        for t in range(MOBA_TOPK):
            for r in range(ppb):
                v_copy(h, t, r).wait()

    pos_h = lax.broadcasted_iota(jnp.int32, (HEAD_DIM, page), 1)
    cols = []
    for h in range(N_HEADS):
        rows = slice(h * HEAD_DIM, (h + 1) * HEAD_DIM)
        acc = p_self[h:h + 1, :] * jnp.where(pos_h == 0, vn_col[rows], 0.0)
        for t in range(MOBA_TOPK):
            for r in range(ppb):
                p_row = s_sc[clamped[h][t] * ppb + r, pl.ds(h, 1), :]
                p_row = jnp.where(blocks[h][t] < n_past, p_row, 0.0)
                acc = acc + p_row * vbuf[(h * MOBA_TOPK + t) * ppb + r]
        cols.append(jnp.sum(acc, axis=1, keepdims=True))
    o_col = jnp.concatenate(cols, axis=0)
    seq_id = lax.broadcasted_iota(jnp.int32, o_ref.shape, 1)
    o_ref[...] = jnp.where(seq_id == b, o_col, o_ref[...])


def _sample_attention(q_t, k_new_t, v_new_t, cache_k, cache_v, page_table):
    s, n_pages = page_table.shape
    n_phys, page = cache_k.shape[0], cache_k.shape[1]
    assert page == LANES and MOBA_BLOCK % page == 0 and (n_pages * page) % MOBA_BLOCK == 0
    n_past = n_pages * page // MOBA_BLOCK
    assert 1 <= n_past < LANES and s <= LANES
    ck = cache_k.transpose(0, 2, 3, 1).reshape(n_phys, ATTN_W, page)
    cv = cache_v.transpose(0, 2, 3, 1).reshape(n_phys, ATTN_W, page)
    n_vbuf = N_HEADS * MOBA_TOPK * (MOBA_BLOCK // page)
    chunk_pages = min(SAMPLE_CHUNK_PAGES, n_pages)
    assert n_pages % chunk_pages == 0
    n_slots = SAMPLE_SEQS_IN_FLIGHT * (n_pages // chunk_pages)
    tok = pl.BlockSpec((ATTN_W, s), lambda b, pt: (0, 0))
    grid_spec = pltpu.PrefetchScalarGridSpec(
        num_scalar_prefetch=1,
        grid=(s,),
        in_specs=[tok, tok, tok, pl.BlockSpec(memory_space=pl.ANY), pl.BlockSpec(memory_space=pl.ANY)],
        out_specs=tok,
        scratch_shapes=[
            pltpu.VMEM((n_slots, chunk_pages, ATTN_W, page), F32),
            pltpu.VMEM((n_vbuf, HEAD_DIM, page), F32),
            pltpu.VMEM((ATTN_W, page), F32),
            pltpu.VMEM((n_pages, N_HEADS, page), F32),
            pltpu.SemaphoreType.DMA((n_slots,)),
            pltpu.SemaphoreType.DMA(()),
        ],
    )
    return pl.pallas_call(
        functools.partial(_sample_attn_kernel, n_pages=n_pages, page=page),
        grid_spec=grid_spec,
        out_shape=jax.ShapeDtypeStruct((ATTN_W, s), F32),
        compiler_params=pltpu.CompilerParams(dimension_semantics=("arbitrary",), vmem_limit_bytes=VMEM_LIMIT),
        name="sample_attn",
    )(page_table, q_t, k_new_t, v_new_t, ck, cv)


def _merge_kernel(x_ref, oa_ref, u_ref, vv_ref, sga_ref, sgb_ref, ws_ref, bs_ref, wpa_ref, wpb_ref, wo_ref,
                  x1_ref, *, single_position):
    tm = x_ref.shape[0]
    if single_position:
        s = vv_ref[...].astype(F32) * ws_ref[...] + bs_ref[...]
    else:
        t_idx = lax.broadcasted_iota(jnp.int32, (SGU_CHUNK, SGU_CHUNK), 0)
        s_idx = lax.broadcasted_iota(jnp.int32, (SGU_CHUNK, SGU_CHUNK), 1)
        ws = [jnp.where(s_idx <= t_idx, ws_ref[g], 0.0).astype(BF16) for g in range(SGU_GROUPS)]
        lane = lax.broadcasted_iota(jnp.int32, (SGU_CHUNK, LANES), 1)
        groups_per_block = LANES // SGU_GROUP_W
        chunks = []
        for c in range(tm // SGU_CHUNK):
            parts = []
            for blk in range(SGU_W // LANES):
                vp = vv_ref[c * SGU_CHUNK:(c + 1) * SGU_CHUNK, blk * LANES:(blk + 1) * LANES]
                sp = jnp.zeros((SGU_CHUNK, LANES), F32)
                for g in range(groups_per_block):
                    sg = jnp.dot(ws[blk * groups_per_block + g], vp, preferred_element_type=F32)
                    sp = jnp.where(lane // SGU_GROUP_W == g, sg, sp)
                parts.append(sp)
            chunks.append(jnp.concatenate(parts, axis=1) + bs_ref[...])
        s = jnp.concatenate(chunks, axis=0)
    ob = (u_ref[...].astype(F32) * s).astype(BF16)
    ba = jnp.dot(oa_ref[...], wpa_ref[...], preferred_element_type=F32)
    bb = jnp.dot(ob, wpb_ref[...], preferred_element_type=F32)
    merged = (sga_ref[...].astype(F32) * ba + sgb_ref[...].astype(F32) * bb).astype(BF16)
    x1_ref[...] = x_ref[...] + jnp.dot(merged, wo_ref[...], preferred_element_type=F32)


def _merge(x2d, o_a, u, vv, sga, sgb, w_s, b_s, w_pa_bf, w_pb_bf, w_o_bf, *, tm, single_position):
    m, d = x2d.shape
    if single_position:
        ws_in = jnp.repeat(w_s[:, 0, 0], SGU_GROUP_W).reshape(1, SGU_W)
        bs_in = jnp.repeat(b_s[:, 0], SGU_GROUP_W).reshape(1, SGU_W)
    else:
        ws_in = w_s
        bs_in = jnp.repeat(b_s.T, SGU_GROUP_W, axis=1)
    row = lambda w: pl.BlockSpec((tm, w), lambda r: (r, 0))
    return pl.pallas_call(
        functools.partial(_merge_kernel, single_position=single_position),
        grid=(m // tm,),
        in_specs=[row(d), row(ATTN_W), row(SGU_W), row(SGU_W), row(d), row(d),
                  _const_spec(ws_in.shape), _const_spec(bs_in.shape),
                  _const_spec(w_pa_bf.shape), _const_spec(w_pb_bf.shape), _const_spec(w_o_bf.shape)],
        out_specs=row(d),
        out_shape=jax.ShapeDtypeStruct((m, d), F32),
        compiler_params=pltpu.CompilerParams(dimension_semantics=("arbitrary",), vmem_limit_bytes=VMEM_LIMIT),
        name="merge",
    )(x2d, o_a, u, vv, sga, sgb, ws_in, bs_in, w_pa_bf, w_pb_bf, w_o_bf)


FFN_CHUNK = 1024


def _ffn_kernel(x_ref, g_ref, wup_ref, wdn_ref, gf_ref, y_ref):
    x = x_ref[...]
    h = _rmsnorm(x, g_ref[...]).astype(BF16)
    acc = x
    for c in range(wup_ref.shape[1] // FFN_CHUNK):
        a = jnp.dot(h, wup_ref[:, c * FFN_CHUNK:(c + 1) * FFN_CHUNK], preferred_element_type=F32)
        a = jnp.square(jnp.maximum(a, 0.0)).astype(BF16)
        acc = acc + jnp.dot(a, wdn_ref[c * FFN_CHUNK:(c + 1) * FFN_CHUNK, :], preferred_element_type=F32)
    y_ref[...] = _rmsnorm(acc, gf_ref[...])


def _ffn(x2d, g_ffn, w_up_bf, w_down_bf, g_final, *, tm):
    m, d = x2d.shape
    row = pl.BlockSpec((tm, d), lambda r: (r, 0))
    return pl.pallas_call(
        _ffn_kernel,
        grid=(m // tm,),
        in_specs=[row, _const_spec((1, d)), _const_spec(w_up_bf.shape), _const_spec(w_down_bf.shape),
                  _const_spec((1, d))],
        out_specs=row,
        out_shape=jax.ShapeDtypeStruct((m, d), F32),
        compiler_params=pltpu.CompilerParams(dimension_semantics=("arbitrary",), vmem_limit_bytes=VMEM_LIMIT),
        name="ffn",
    )(x2d, g_ffn.reshape(1, d), w_up_bf, w_down_bf, g_final.reshape(1, d))


def _rope_tables(pos):
    half = HEAD_DIM // 2
    inv = ROPE_THETA ** (-2.0 * jnp.arange(half, dtype=F32) / HEAD_DIM)
    ang = pos.astype(F32)[:, None] * inv[None, :]
    cos = jnp.tile(jnp.cos(ang), (1, 2 * N_HEADS))
    sin = jnp.tile(jnp.concatenate([-jnp.sin(ang), jnp.sin(ang)], axis=1), (1, N_HEADS))
    return cos, sin


def kernel(x_prompt, x_sample, cache_k, cache_v, page_table, g_attn, w_in, w_pa, w_pb, w_o, g_v, w_s, b_s,
           g_ffn, w_up, w_down, g_final):
    depth = w_in.shape[0]
    assert depth == 1, "the final norm is fused into the last layer's MLP kernel"
    bsz, seq, d = x_prompt.shape
    n_dec, dec_seq, _ = x_sample.shape
    assert dec_seq == 1 and seq % MOBA_BLOCK == 0
    past_len = page_table.shape[1] * cache_k.shape[2]
    n_chunk_rows = seq - ((seq - 1) // SGU_CHUNK) * SGU_CHUNK
    assert n_chunk_rows == SGU_CHUNK

    cos_p, sin_p = _rope_tables(jnp.arange(seq, dtype=jnp.int32))
    cos_s, sin_s = _rope_tables(jnp.full((n_dec,), past_len, dtype=jnp.int32))

    xp = x_prompt.reshape(bsz * seq, d)
    xs = x_sample.reshape(n_dec, d)
    l = 0
    w_in_bf, w_pa_bf, w_pb_bf, w_o_bf = (w[l].astype(BF16) for w in (w_in, w_pa, w_pb, w_o))
    w_up_bf, w_down_bf = w_up[l].astype(BF16), w_down[l].astype(BF16)

    tm_p = MOBA_BLOCK
    (q_t, k_p, v_p, kb, vb_t, u, vv, sga, sgb, sguv_p, kmean) = _inproj(
        xp, g_attn[l], w_in_bf, cos_p, sin_p, g_v[l], tm=tm_p, seq_tiles=seq // tm_p, q_dtype=BF16,
        q_scale=PROMPT_Q_SCALE, emit_kmean=True)
    o_a = _prompt_attention(q_t, kb.reshape(bsz, seq, ATTN_W), vb_t,
                            kmean.reshape(bsz, seq // MOBA_BLOCK, ATTN_W)).reshape(bsz * seq, ATTN_W)
    x1 = _merge(xp, o_a, u, vv, sga, sgb, w_s[l], b_s[l], w_pa_bf, w_pb_bf, w_o_bf, tm=512, single_position=False)
    y_p = _ffn(x1, g_ffn[l], w_up_bf, w_down_bf, g_final, tm=512)

    (q_s, k_s, v_s, _, _, u_s, vv_s, sga_s, sgb_s, sguv_s) = _inproj(
        xs, g_attn[l], w_in_bf, cos_s, sin_s, g_v[l], tm=n_dec, seq_tiles=1, q_dtype=F32, q_scale=SCALE,
        emit_kmean=False)
    o_as = _sample_attention(q_s[0], k_s[0], v_s[0], cache_k[l], cache_v[l], page_table).T.astype(BF16)
    x1_s = _merge(xs, o_as, u_s, vv_s, sga_s, sgb_s, w_s[l], b_s[l], w_pa_bf, w_pb_bf, w_o_bf,
                  tm=n_dec, single_position=True)
    y_s = _ffn(x1_s, g_ffn[l], w_up_bf, w_down_bf, g_final, tm=n_dec)

    def heads_last(t, n, length):
        return t.reshape(1, n, N_HEADS, HEAD_DIM, length).transpose(0, 1, 4, 2, 3)

    return (y_p.reshape(bsz, seq, d),
            y_s.reshape(n_dec, 1, d),
            heads_last(k_p, bsz, seq),
            heads_last(v_p, bsz, seq),
            sguv_p.reshape(1, bsz, SGU_CHUNK, SGU_W),
            heads_last(k_s, 1, n_dec).reshape(1, n_dec, 1, N_HEADS, HEAD_DIM),
            heads_last(v_s, 1, n_dec).reshape(1, n_dec, 1, N_HEADS, HEAD_DIM),
            sguv_s.reshape(1, n_dec, 1, SGU_W))
```

```python
import functools

import jax
import jax.numpy as jnp
from jax import lax
from jax.experimental import pallas as pl
from jax.experimental.pallas import tpu as pltpu

N_HEADS = 8
HEAD_DIM = 64
ATTN_W = N_HEADS * HEAD_DIM
MOBA_BLOCK = 256
MOBA_TOPK = 3
SGU_GROUPS = 8
SGU_W = 512
SGU_GROUP_W = SGU_W // SGU_GROUPS
SGU_CHUNK = 128
ROPE_THETA = 10000.0
EPS = 1e-6
NEG = -1e30
SCALE = HEAD_DIM ** -0.5
LOG2E = 1.4426950408889634
PROMPT_Q_SCALE = SCALE * LOG2E

LANES = 128
F32_SUBLANES = 8
BF16_SUBLANES = 16
HEADS_PER_LANE_BLOCK = LANES // HEAD_DIM
VMEM_LIMIT = 56 * 1024 * 1024
PROMPT_ROW_TILE = 2 * MOBA_BLOCK
DMA_PRIORITIES = 2

F32 = jnp.float32
BF16 = jnp.bfloat16
NT_DIMS = (((1,), (1,)), ((), ()))


def _rmsnorm(x, g):
    return x * lax.rsqrt(jnp.mean(x * x, axis=-1, keepdims=True) + EPS) * g


def _const_spec(shape):
    return pl.BlockSpec(shape, lambda *_: (0,) * len(shape), pipeline_mode=pl.Buffered(1))


def _rope(t, cos, sin_signed):
    lane = lax.broadcasted_iota(jnp.int32, (t.shape[0], LANES), 1)
    first_half = (lane & (HEAD_DIM - 1)) < HEAD_DIM // 2
    outs = []
    for c in range(t.shape[1] // LANES):
        sl = slice(c * LANES, (c + 1) * LANES)
        tc = t[:, sl]
        partner = jnp.where(first_half, pltpu.roll(tc, LANES - HEAD_DIM // 2, 1),
                            pltpu.roll(tc, HEAD_DIM // 2, 1))
        outs.append(tc * cos[:, sl] + partner * sin_signed[:, sl])
    return jnp.concatenate(outs, axis=1)


def _inproj_kernel(x_ref, g_ref, w_ref, cos_ref, sin_ref, gv_ref,
                   q_ref, kf_ref, vf_ref, kb_ref, vb_ref, u_ref, vv_ref, sga_ref, sgb_ref, sguv_ref,
                   *maybe_kmean_and_scratch, q_scale):
    *maybe_kmean_ref, v_sc = maybe_kmean_and_scratch
    d_model = x_ref.shape[1]
    h = _rmsnorm(x_ref[...], g_ref[...]).astype(BF16)

    def proj(lo, width):
        return jnp.dot(h, w_ref[:, lo:lo + width], preferred_element_type=F32)

    cos = cos_ref[...]
    sin = sin_ref[...]
    q = _rope(proj(0, ATTN_W), cos, sin) * q_scale
    q_ref[0] = q.T.astype(q_ref.dtype)
    k = _rope(proj(ATTN_W, ATTN_W), cos, sin)
    kf_ref[0] = k.T
    kb_ref[...] = k.astype(BF16)
    if maybe_kmean_ref:
        blocks = k.shape[0] // MOBA_BLOCK
        maybe_kmean_ref[0][:, 0, :] = jnp.mean(k.reshape(blocks, MOBA_BLOCK, ATTN_W), axis=1)
    v_sc[...] = proj(2 * ATTN_W, ATTN_W)
    v_t = v_sc[...].T
    vf_ref[0] = v_t
    vb_ref[0] = v_t.astype(BF16)
    base = 3 * ATTN_W
    u_ref[...] = jax.nn.gelu(proj(base, SGU_W)).astype(BF16)
    vv = _rmsnorm(jax.nn.gelu(proj(base + SGU_W, SGU_W)), gv_ref[...])
    vv_ref[...] = vv.astype(BF16)
    sguv_ref[...] = vv[vv.shape[0] - sguv_ref.shape[0]:, :]
    base += 2 * SGU_W
    sga_ref[...] = jax.nn.sigmoid(proj(base, d_model)).astype(BF16)
    sgb_ref[...] = jax.nn.sigmoid(proj(base + d_model, d_model)).astype(BF16)


def _inproj(x2d, g_attn, w_in_bf, cos_tab, sin_tab, g_v, *, tm, seq_tiles, q_dtype, q_scale, emit_kmean):
    m, d = x2d.shape
    n_tiles = m // tm
    n_seq = n_tiles // seq_tiles
    keep = min(SGU_CHUNK, tm)
    row = lambda w: pl.BlockSpec((tm, w), lambda r: (r, 0))
    tab = pl.BlockSpec((tm, ATTN_W), lambda r: (r % seq_tiles, 0))
    chan = pl.BlockSpec((1, ATTN_W, tm), lambda r: (r // seq_tiles, 0, r % seq_tiles))
    chan_shape = lambda dt: jax.ShapeDtypeStruct((n_seq, ATTN_W, seq_tiles * tm), dt)
    out_shape = [
        chan_shape(q_dtype),
        chan_shape(F32),
        chan_shape(F32),
        jax.ShapeDtypeStruct((m, ATTN_W), BF16),
        chan_shape(BF16),
        jax.ShapeDtypeStruct((m, SGU_W), BF16),
        jax.ShapeDtypeStruct((m, SGU_W), BF16),
        jax.ShapeDtypeStruct((m, d), BF16),
        jax.ShapeDtypeStruct((m, d), BF16),
        jax.ShapeDtypeStruct((n_seq * keep, SGU_W), F32),
    ]
    out_specs = [chan, chan, chan, row(ATTN_W), chan] + [row(SGU_W)] * 2 + [row(d)] * 2 + [
        pl.BlockSpec((keep, SGU_W), lambda r: (r // seq_tiles, 0))]
    if emit_kmean:
        assert tm % MOBA_BLOCK == 0
        out_shape.append(jax.ShapeDtypeStruct((m // MOBA_BLOCK, 1, ATTN_W), F32))
        out_specs.append(pl.BlockSpec((tm // MOBA_BLOCK, 1, ATTN_W), lambda r: (r, 0, 0)))
    return pl.pallas_call(
        functools.partial(_inproj_kernel, q_scale=q_scale),
        grid=(n_tiles,),
        in_specs=[row(d), _const_spec((1, d)), _const_spec(w_in_bf.shape), tab, tab, _const_spec((1, SGU_W))],
        out_specs=out_specs,
        out_shape=out_shape,
        scratch_shapes=[pltpu.VMEM((tm, ATTN_W), F32)],
        compiler_params=pltpu.CompilerParams(dimension_semantics=("arbitrary",), vmem_limit_bytes=VMEM_LIMIT),
        name="inproj",
    )(x2d, g_attn.reshape(1, d), w_in_bf, cos_tab, sin_tab, g_v.reshape(1, SGU_W))


def _moba_bias(gate, tile, n_blocks):
    blk = lax.broadcasted_iota(jnp.int32, gate.shape, 0).astype(F32)
    neg = NEG * PROMPT_Q_SCALE
    gate = jnp.where(blk < tile, gate, jnp.where(blk < n_blocks, neg, -jnp.inf))
    if tile <= MOBA_TOPK:
        return jnp.where(gate >= neg, 0.0, jnp.where(blk < tile, NEG, 0.0))
    cand = gate
    for t in range(MOBA_TOPK):
        best = jnp.max(cand, axis=0, keepdims=True)
        idx = jnp.min(jnp.where(cand == best, blk, float(gate.shape[0])), axis=0, keepdims=True)
        if t + 1 < MOBA_TOPK:
            cand = jnp.where(blk == idx, -jnp.inf, cand)
    bias = jnp.where(gate > best, 0.0, jnp.where(gate == best, jnp.where(blk <= idx, 0.0, NEG), NEG))
    return jnp.where(blk >= tile, 0.0, bias)


def _attn_tile(q_ref, k_ref, v_ref, km_ref, o_ref, s_sc, tile):
    tq = MOBA_BLOCK
    g_n = HEADS_PER_LANE_BLOCK
    cols = g_n * tq
    n_blocks = km_ref.shape[1]
    q_t = q_ref[0]
    chan = lax.broadcasted_iota(jnp.int32, (LANES, tq), 0)
    qs = jnp.concatenate([jnp.where(chan // HEAD_DIM == g, q_t, jnp.zeros_like(q_t)) for g in range(g_n)], axis=1)

    bias = None
    if tile > 0:
        pad_rows = -n_blocks % BF16_SUBLANES
        km = jnp.concatenate([km_ref[0], jnp.zeros((pad_rows, LANES), F32)], axis=0)
        km_hi = km.astype(BF16)
        km_lo = (km - km_hi.astype(F32)).astype(BF16)
        gate = (jnp.dot(km_hi, qs, preferred_element_type=F32) + jnp.dot(km_lo, qs, preferred_element_type=F32))
        bias = _moba_bias(gate, tile, n_blocks)

    s_all = jnp.dot(k_ref[0, 0:(tile + 1) * MOBA_BLOCK, :], qs, preferred_element_type=F32)
    kpos = lax.broadcasted_iota(jnp.int32, (MOBA_BLOCK, cols), 0)
    qpos = lax.broadcasted_iota(jnp.int32, (MOBA_BLOCK, cols), 1) & (tq - 1)
    groups = MOBA_BLOCK // F32_SUBLANES
    run_max = None
    for j in range(tile + 1):
        s = s_all[j * MOBA_BLOCK:(j + 1) * MOBA_BLOCK]
        if j == tile:
            s = jnp.where(kpos <= qpos, s, NEG)
        s_sc[j] = s
        part = jnp.max(s.reshape(groups, F32_SUBLANES, cols), axis=0)
        if j < tile:
            part = part + bias[j:j + 1, :]
        run_max = part if run_max is None else jnp.maximum(run_max, part)
    m = jnp.max(run_max, axis=0, keepdims=True)

    acc = None
    run_sum = None
    for j in range(tile + 1):
        shift = m - bias[j:j + 1, :] if j < tile else m
        p = jnp.exp2(s_sc[j] - shift)
        part = jnp.sum(p.reshape(groups, F32_SUBLANES, cols), axis=0)
        run_sum = part if run_sum is None else run_sum + part
        pv = jnp.dot(v_ref[0, :, j * MOBA_BLOCK:(j + 1) * MOBA_BLOCK], p.astype(BF16), preferred_element_type=F32)
        acc = pv if acc is None else acc + pv
    out = acc / jnp.sum(run_sum, axis=0, keepdims=True)
    o_t = jnp.zeros((LANES, tq), F32)
    for g in range(g_n):
        o_t = jnp.where(chan // HEAD_DIM == g, out[:, g * tq:(g + 1) * tq], o_t)
    o_ref[0] = o_t.T.astype(o_ref.dtype)


def _prompt_attn_kernel(q_ref, k_ref, v_ref, km_ref, o_ref, s_sc, *, n_blocks):
    i = pl.program_id(2)
    for tile in range(n_blocks):
        @pl.when(i == tile)
        def _(tile=tile):
            _attn_tile(q_ref, k_ref, v_ref, km_ref, o_ref, s_sc, tile)


def _prompt_attention(q_t, k, v_t, kmean):
    b, l, _ = k.shape
    n_blocks = l // MOBA_BLOCK
    assert n_blocks <= LANES
    cols = HEADS_PER_LANE_BLOCK * MOBA_BLOCK
    return pl.pallas_call(
        functools.partial(_prompt_attn_kernel, n_blocks=n_blocks),
        grid=(b, ATTN_W // LANES, n_blocks),
        in_specs=[
            pl.BlockSpec((1, LANES, MOBA_BLOCK), lambda bi, c, i: (bi, c, i)),
            pl.BlockSpec((1, l, LANES), lambda bi, c, i: (bi, 0, c)),
            pl.BlockSpec((1, LANES, l), lambda bi, c, i: (bi, c, 0)),
            pl.BlockSpec((1, n_blocks, LANES), lambda bi, c, i: (bi, 0, c)),
        ],
        out_specs=pl.BlockSpec((1, MOBA_BLOCK, LANES), lambda bi, c, i: (bi, i, c)),
        out_shape=jax.ShapeDtypeStruct((b, l, ATTN_W), BF16),
        scratch_shapes=[pltpu.VMEM((n_blocks, MOBA_BLOCK, cols), F32)],
        compiler_params=pltpu.CompilerParams(
            dimension_semantics=("arbitrary", "arbitrary", "arbitrary"), vmem_limit_bytes=VMEM_LIMIT),
        name="prompt_attn",
    )(q_t, k, v_t, kmean)


def _sample_attn_kernel(pt_ref, q_ref, kn_ref, vn_ref, ck_hbm, cv_hbm, o_ref,
                        kbuf, vbuf, qb_sc, s_sc, sem_k, sem_v, *, n_pages, page):
    b = pl.program_id(0)
    slot = b % 2
    ppb = MOBA_BLOCK // page
    n_past = n_pages // ppb

    def k_copy(seq, sl, p):
        return pltpu.make_async_copy(ck_hbm.at[pt_ref[seq, p]], kbuf.at[sl, p], sem_k.at[sl])

    def start_keys(seq, sl):
        for p in range(n_pages):
            k_copy(seq, sl, p).start(priority=p % DMA_PRIORITIES)

    @pl.when(b == 0)
    def _():
        o_ref[...] = jnp.zeros_like(o_ref)
        start_keys(0, 0)

    for p in range(n_pages):
        k_copy(b, slot, p).wait()

    @pl.when(b + 1 < pl.num_programs(0))
    def _():
        start_keys(b + 1, 1 - slot)

    def column(ref):
        a = ref[...]
        seq_id = lax.broadcasted_iota(jnp.int32, a.shape, 1)
        return jnp.sum(jnp.where(seq_id == b, a, 0.0), axis=1, keepdims=True)

    q_col, kn_col, vn_col = column(q_ref), column(kn_ref), column(vn_ref)
    qb_sc[...] = jnp.broadcast_to(q_col, qb_sc.shape)

    def page_scores(kt):
        return jnp.sum((kt * qb_sc[...]).reshape(N_HEADS, HEAD_DIM, page), axis=1)

    def score_body(p, carry):
        s_sc[p] = page_scores(kbuf[slot, p])
        return carry
    lax.fori_loop(0, n_pages, score_body, 0, unroll=2)

    pos_w = lax.broadcasted_iota(jnp.int32, (ATTN_W, page), 1)
    s_self = page_scores(jnp.where(pos_w == 0, kn_col, 0.0))

    hl = lax.broadcasted_iota(jnp.int32, (N_HEADS, LANES), 1)
    cand = jnp.where(hl == n_past, NEG * SCALE, -jnp.inf).astype(F32)
    for j in range(n_past):
        blk = s_sc[j * ppb]
        for r in range(1, ppb):
            blk = blk + s_sc[j * ppb + r]
        cand = jnp.where(hl == j, jnp.sum(blk, axis=1, keepdims=True) * (1.0 / MOBA_BLOCK), cand)
    rank = jnp.zeros((N_HEADS, LANES), F32)
    for j in range(n_past + 1):
        gj = cand[:, j:j + 1]
        rank = rank + jnp.where(gj > cand, 1.0, jnp.where(gj == cand, jnp.where(hl > j, 1.0, 0.0), 0.0))
    hl_f = hl.astype(F32)
    picks = [jnp.sum(jnp.where(rank == t, hl_f, 0.0), axis=1, keepdims=True).astype(jnp.int32)
             for t in range(MOBA_TOPK)]

    blocks = [[picks[t][h, 0] for t in range(MOBA_TOPK)] for h in range(N_HEADS)]
    clamped = [[jnp.minimum(blk, n_past - 1) for blk in row] for row in blocks]

    def v_copy(h, t, r):
        phys = pt_ref[b, clamped[h][t] * ppb + r]
        return pltpu.make_async_copy(cv_hbm.at[phys, pl.ds(h * HEAD_DIM, HEAD_DIM), :],
                                     vbuf.at[(h * MOBA_TOPK + t) * ppb + r], sem_v)

    for h in range(N_HEADS):
        for t in range(MOBA_TOPK):
            for r in range(ppb):
                v_copy(h, t, r).start(priority=(h * MOBA_TOPK + t) % DMA_PRIORITIES)

    s_all = s_sc[...]
    blk_id = lax.broadcasted_iota(jnp.int32, s_all.shape, 0) // ppb
    chosen = blk_id == picks[0][None]
    for t in range(1, MOBA_TOPK):
        chosen = chosen | (blk_id == picks[t][None])
    s_all = jnp.where(chosen, s_all, NEG)
    s_self = jnp.where(hl == 0, s_self, NEG)
    m = jnp.max(jnp.maximum(jnp.max(s_all, axis=0), s_self), axis=1, keepdims=True)
    p_all = jnp.exp(s_all - m[None])
    p_self = jnp.exp(s_self - m)
    denom = jnp.sum(jnp.sum(p_all, axis=0) + p_self, axis=1, keepdims=True)
    s_sc[...] = p_all / denom[None]
    p_self = p_self / denom

    for h in range(N_HEADS):
        for t in range(MOBA_TOPK):
            for r in range(ppb):
                v_copy(h, t, r).wait()

    pos_h = lax.broadcasted_iota(jnp.int32, (HEAD_DIM, page), 1)
    cols = []
    for h in range(N_HEADS):
        rows = slice(h * HEAD_DIM, (h + 1) * HEAD_DIM)
        acc = p_self[h:h + 1, :] * jnp.where(pos_h == 0, vn_col[rows], 0.0)
        for t in range(MOBA_TOPK):
            for r in range(ppb):
                p_row = s_sc[clamped[h][t] * ppb + r, pl.ds(h, 1), :]
                p_row = jnp.where(blocks[h][t] < n_past, p_row, 0.0)
                acc = acc + p_row * vbuf[(h * MOBA_TOPK + t) * ppb + r]
        cols.append(jnp.sum(acc, axis=1, keepdims=True))
    o_col = jnp.concatenate(cols, axis=0)
    seq_id = lax.broadcasted_iota(jnp.int32, o_ref.shape, 1)
    o_ref[...] = jnp.where(seq_id == b, o_col, o_ref[...])


def _sample_attention(q_t, k_new_t, v_new_t, cache_k, cache_v, page_table):
    s, n_pages = page_table.shape
    n_phys, page = cache_k.shape[0], cache_k.shape[1]
    assert page == LANES and MOBA_BLOCK % page == 0 and (n_pages * page) % MOBA_BLOCK == 0
    n_past = n_pages * page // MOBA_BLOCK
    assert 1 <= n_past < LANES and s <= LANES
    ck = cache_k.transpose(0, 2, 3, 1).reshape(n_phys, ATTN_W, page)
    cv = cache_v.transpose(0, 2, 3, 1).reshape(n_phys, ATTN_W, page)
    n_vbuf = N_HEADS * MOBA_TOPK * (MOBA_BLOCK // page)
    tok =pl.BlockSpec((ATTN_W, s), lambda b, pt: (0, 0))
    grid_spec = pltpu.PrefetchScalarGridSpec(
        num_scalar_prefetch=1,
        grid=(s,),
        in_specs=[tok, tok, tok, pl.BlockSpec(memory_space=pl.ANY), pl.BlockSpec(memory_space=pl.ANY)],
        out_specs=tok,
        scratch_shapes=[
            pltpu.VMEM((2, n_pages, ATTN_W, page), F32),
            pltpu.VMEM((n_vbuf, HEAD_DIM, page), F32),
            pltpu.VMEM((ATTN_W, page), F32),
            pltpu.VMEM((n_pages, N_HEADS, page), F32),
            pltpu.SemaphoreType.DMA((2,)),
            pltpu.SemaphoreType.DMA(()),
        ],
    )
    return pl.pallas_call(
        functools.partial(_sample_attn_kernel, n_pages=n_pages, page=page),
        grid_spec=grid_spec,
        out_shape=jax.ShapeDtypeStruct((ATTN_W, s), F32),
        compiler_params=pltpu.CompilerParams(dimension_semantics=("arbitrary",), vmem_limit_bytes=VMEM_LIMIT),
        name="sample_attn",
    )(page_table, q_t, k_new_t, v_new_t, ck, cv)


def _merge_kernel(x_ref, oa_ref, u_ref, vv_ref, sga_ref, sgb_ref, ws_ref, bs_ref, wpa_ref, wpb_ref, wo_ref,
                  x1_ref, *, single_position):
    tm = x_ref.shape[0]
    if single_position:
        s = vv_ref[...].astype(F32) * ws_ref[...] + bs_ref[...]
    else:
        t_idx = lax.broadcasted_iota(jnp.int32, (SGU_CHUNK, SGU_CHUNK), 0)
        s_idx = lax.broadcasted_iota(jnp.int32, (SGU_CHUNK, SGU_CHUNK), 1)
        ws = [jnp.where(s_idx <= t_idx, ws_ref[g], 0.0).astype(BF16) for g in range(SGU_GROUPS)]
        lane = lax.broadcasted_iota(jnp.int32, (SGU_CHUNK, LANES), 1)
        groups_per_block = LANES // SGU_GROUP_W
        chunks = []
        for c in range(tm // SGU_CHUNK):
            parts = []
            for blk in range(SGU_W // LANES):
                vp = vv_ref[c * SGU_CHUNK:(c + 1) * SGU_CHUNK, blk * LANES:(blk + 1) * LANES]
                sp = jnp.zeros((SGU_CHUNK, LANES), F32)
                for g in range(groups_per_block):
                    sg = jnp.dot(ws[blk * groups_per_block + g], vp, preferred_element_type=F32)
                    sp = jnp.where(lane // SGU_GROUP_W == g, sg, sp)
                parts.append(sp)
            chunks.append(jnp.concatenate(parts, axis=1) + bs_ref[...])
        s = jnp.concatenate(chunks, axis=0)
    ob = (u_ref[...].astype(F32) * s).astype(BF16)
    ba = jnp.dot(oa_ref[...], wpa_ref[...], preferred_element_type=F32)
    bb = jnp.dot(ob, wpb_ref[...], preferred_element_type=F32)
    merged = (sga_ref[...].astype(F32) * ba + sgb_ref[...].astype(F32) * bb).astype(BF16)
    x1_ref[...] = x_ref[...] + jnp.dot(merged, wo_ref[...], preferred_element_type=F32)


def _merge(x2d, o_a, u, vv, sga, sgb, w_s, b_s, w_pa_bf, w_pb_bf, w_o_bf, *, tm, single_position):
    m, d = x2d.shape
    if single_position:
        ws_in = jnp.repeat(w_s[:, 0, 0], SGU_GROUP_W).reshape(1, SGU_W)
        bs_in = jnp.repeat(b_s[:, 0], SGU_GROUP_W).reshape(1, SGU_W)
    else:
        ws_in = w_s
        bs_in = jnp.repeat(b_s.T, SGU_GROUP_W, axis=1)
    row = lambda w: pl.BlockSpec((tm, w), lambda r: (r, 0))
    return pl.pallas_call(
        functools.partial(_merge_kernel, single_position=single_position),
        grid=(m // tm,),
        in_specs=[row(d), row(ATTN_W), row(SGU_W), row(SGU_W), row(d), row(d),
                  _const_spec(ws_in.shape), _const_spec(bs_in.shape),
                  _const_spec(w_pa_bf.shape), _const_spec(w_pb_bf.shape), _const_spec(w_o_bf.shape)],
        out_specs=row(d),
        out_shape=jax.ShapeDtypeStruct((m, d), F32),
        compiler_params=pltpu.CompilerParams(dimension_semantics=("arbitrary",), vmem_limit_bytes=VMEM_LIMIT),
        name="merge",
    )(x2d, o_a, u, vv, sga, sgb, ws_in, bs_in, w_pa_bf, w_pb_bf, w_o_bf)


FFN_CHUNK = 1024


def _ffn_kernel(x_ref, g_ref, wup_ref, wdn_ref, gf_ref, y_ref):
    x = x_ref[...]
    h = _rmsnorm(x, g_ref[...]).astype(BF16)
    acc = x
    for c in range(wup_ref.shape[1] // FFN_CHUNK):
        a = jnp.dot(h, wup_ref[:, c * FFN_CHUNK:(c + 1) * FFN_CHUNK], preferred_element_type=F32)
        a = jnp.square(jnp.maximum(a, 0.0)).astype(BF16)
        acc = acc + jnp.dot(a, wdn_ref[c * FFN_CHUNK:(c + 1) * FFN_CHUNK, :], preferred_element_type=F32)
    y_ref[...] = _rmsnorm(acc, gf_ref[...])


def _ffn(x2d, g_ffn, w_up_bf, w_down_bf, g_final, *, tm):
    m, d = x2d.shape
    row = pl.BlockSpec((tm, d), lambda r: (r, 0))
    return pl.pallas_call(
        _ffn_kernel,
        grid=(m // tm,),
        in_specs=[row, _const_spec((1, d)), _const_spec(w_up_bf.shape), _const_spec(w_down_bf.shape),
                  _const_spec((1, d))],
        out_specs=row,
        out_shape=jax.ShapeDtypeStruct((m, d), F32),
        compiler_params=pltpu.CompilerParams(dimension_semantics=("arbitrary",), vmem_limit_bytes=VMEM_LIMIT),
        name="ffn",
    )(x2d, g_ffn.reshape(1, d), w_up_bf, w_down_bf, g_final.reshape(1, d))


def _rope_tables(pos):
    half = HEAD_DIM // 2
    inv = ROPE_THETA ** (-2.0 * jnp.arange(half, dtype=F32) / HEAD_DIM)
    ang = pos.astype(F32)[:, None] * inv[None, :]
    cos = jnp.tile(jnp.cos(ang), (1, 2 * N_HEADS))
    sin = jnp.tile(jnp.concatenate([-jnp.sin(ang), jnp.sin(ang)], axis=1), (1, N_HEADS))
    return cos, sin


def kernel(x_prompt, x_sample, cache_k, cache_v, page_table, g_attn, w_in, w_pa, w_pb, w_o, g_v, w_s, b_s,
           g_ffn, w_up, w_down, g_final):
    depth = w_in.shape[0]
    assert depth == 1, "the final norm is fused into the last layer's MLP kernel"
    bsz, seq, d = x_prompt.shape
    n_dec, dec_seq, _ = x_sample.shape
    assert dec_seq == 1 and seq % MOBA_BLOCK == 0
    past_len = page_table.shape[1] * cache_k.shape[2]
    n_chunk_rows = seq - ((seq - 1) // SGU_CHUNK) * SGU_CHUNK
    assert n_chunk_rows == SGU_CHUNK

    cos_p, sin_p = _rope_tables(jnp.arange(seq, dtype=jnp.int32))
    cos_s, sin_s = _rope_tables(jnp.full((n_dec,), past_len, dtype=jnp.int32))

    xp = x_prompt.reshape(bsz * seq, d)
    xs = x_sample.reshape(n_dec, d)
    l = 0
    w_in_bf, w_pa_bf, w_pb_bf, w_o_bf = (w[l].astype(BF16) for w in (w_in, w_pa, w_pb, w_o))
    w_up_bf, w_down_bf = w_up[l].astype(BF16), w_down[l].astype(BF16)

    tm_p = PROMPT_ROW_TILE
    assert seq % tm_p == 0
    (q_t, k_p, v_p, kb, vb_t, u, vv, sga, sgb, sguv_p, kmean) = _inproj(
        xp, g_attn[l], w_in_bf, cos_p, sin_p, g_v[l], tm=tm_p, seq_tiles=seq // tm_p, q_dtype=BF16,
        q_scale=PROMPT_Q_SCALE, emit_kmean=True)
    o_a = _prompt_attention(q_t, kb.reshape(bsz, seq, ATTN_W), vb_t,
                            kmean.reshape(bsz, seq // MOBA_BLOCK, ATTN_W)).reshape(bsz * seq, ATTN_W)
    x1 = _merge(xp, o_a, u, vv, sga, sgb, w_s[l], b_s[l], w_pa_bf, w_pb_bf, w_o_bf, tm=tm_p, single_position=False)
    y_p = _ffn(x1, g_ffn[l], w_up_bf, w_down_bf, g_final, tm=tm_p)

    (q_s, k_s, v_s, _, _, u_s, vv_s, sga_s, sgb_s, sguv_s) = _inproj(
        xs, g_attn[l], w_in_bf, cos_s, sin_s, g_v[l], tm=n_dec, seq_tiles=1, q_dtype=F32, q_scale=SCALE,
        emit_kmean=False)
    o_as = _sample_attention(q_s[0], k_s[0], v_s[0], cache_k[l], cache_v[l], page_table).T.astype(BF16)
    x1_s = _merge(xs, o_as, u_s, vv_s, sga_s, sgb_s, w_s[l], b_s[l], w_pa_bf, w_pb_bf, w_o_bf,
                  tm=n_dec, single_position=True)
    y_s = _ffn(x1_s, g_ffn[l], w_up_bf, w_down_bf, g_final, tm=n_dec)

    def heads_last(t, n, length):
        return t.reshape(1, n, N_HEADS, HEAD_DIM, length).transpose(0, 1, 4, 2, 3)

    return (y_p.reshape(bsz, seq, d),
            y_s.reshape(n_dec, 1, d),
            heads_last(k_p, bsz, seq),
            heads_last(v_p, bsz, seq),
            sguv_p.reshape(1, bsz, SGU_CHUNK, SGU_W),
            heads_last(k_s, 1, n_dec).reshape(1, n_dec, 1, N_HEADS, HEAD_DIM),
            heads_last(v_s, 1, n_dec).reshape(1, n_dec, 1, N_HEADS, HEAD_DIM),
            sguv_s.reshape(1, n_dec, 1, SGU_W))
```

```python
import functools

import jax
import jax.numpy as jnp
from jax import lax
from jax.experimental import pallas as pl
from jax.experimental.pallas import tpu as pltpu

N_HEADS = 8
HEAD_DIM = 64
ATTN_W = N_HEADS * HEAD_DIM
MOBA_BLOCK = 256
MOBA_TOPK = 3
SGU_GROUPS = 8
SGU_W = 512
SGU_GROUP_W = SGU_W // SGU_GROUPS
SGU_CHUNK = 128
ROPE_THETA = 10000.0
EPS = 1e-6
NEG = -1e30
SCALE = HEAD_DIM ** -0.5
LOG2E = 1.4426950408889634
PROMPT_Q_SCALE = SCALE * LOG2E

LANES = 128
F32_SUBLANES = 8
BF16_SUBLANES = 16
HEADS_PER_LANE_BLOCK = LANES // HEAD_DIM
VMEM_LIMIT = 56 * 1024 * 1024
PROMPT_ROW_TILE = 2 * MOBA_BLOCK
SAMPLE_CHUNK_PAGES = 16
DMA_PRIORITIES = 2

F32 = jnp.float32
BF16 = jnp.bfloat16
NT_DIMS = (((1,), (1,)), ((), ()))


def _rmsnorm(x, g):
    return x * lax.rsqrt(jnp.mean(x * x, axis=-1, keepdims=True) + EPS) * g


def _const_spec(shape):
    return pl.BlockSpec(shape, lambda *_: (0,) * len(shape), pipeline_mode=pl.Buffered(1))


def _rope(t, cos, sin_signed):
    lane = lax.broadcasted_iota(jnp.int32, (t.shape[0], LANES), 1)
    first_half = (lane & (HEAD_DIM - 1)) < HEAD_DIM // 2
    outs = []
    for c in range(t.shape[1] // LANES):
        sl = slice(c * LANES, (c + 1) * LANES)
        tc = t[:, sl]
        partner = jnp.where(first_half, pltpu.roll(tc, LANES - HEAD_DIM // 2, 1),
                            pltpu.roll(tc, HEAD_DIM // 2, 1))
        outs.append(tc * cos[:, sl] + partner * sin_signed[:, sl])
    return jnp.concatenate(outs, axis=1)


def _inproj_kernel(x_ref, g_ref, w_ref, cos_ref, sin_ref, gv_ref,
                   q_ref, kf_ref, vf_ref, kb_ref, vb_ref, u_ref, vv_ref, sga_ref, sgb_ref, sguv_ref,
                   *maybe_kmean_and_scratch, q_scale):
    *maybe_kmean_ref, v_sc = maybe_kmean_and_scratch
    d_model = x_ref.shape[1]
    h = _rmsnorm(x_ref[...], g_ref[...]).astype(BF16)

    def proj(lo, width):
        return jnp.dot(h, w_ref[:, lo:lo + width], preferred_element_type=F32)

    cos = cos_ref[...]
    sin = sin_ref[...]
    q = _rope(proj(0, ATTN_W), cos, sin) * q_scale
    q_ref[0] = q.T.astype(q_ref.dtype)
    k = _rope(proj(ATTN_W, ATTN_W), cos, sin)
    kf_ref[0] = k.T
    kb_ref[...] = k.astype(BF16)
    if maybe_kmean_ref:
        blocks = k.shape[0] // MOBA_BLOCK
        maybe_kmean_ref[0][:, 0, :] = jnp.mean(k.reshape(blocks, MOBA_BLOCK, ATTN_W), axis=1)
    v_sc[...] = proj(2 * ATTN_W, ATTN_W)
    v_t = v_sc[...].T
    vf_ref[0] = v_t
    vb_ref[0] = v_t.astype(BF16)
    base = 3 * ATTN_W
    u_ref[...] = jax.nn.gelu(proj(base, SGU_W)).astype(BF16)
    vv = _rmsnorm(jax.nn.gelu(proj(base + SGU_W, SGU_W)), gv_ref[...])
    vv_ref[...] = vv.astype(BF16)
    sguv_ref[...] = vv[vv.shape[0] - sguv_ref.shape[0]:, :]
    base += 2 * SGU_W
    sga_ref[...] = jax.nn.sigmoid(proj(base, d_model)).astype(BF16)
    sgb_ref[...] = jax.nn.sigmoid(proj(base + d_model, d_model)).astype(BF16)


def _inproj(x2d, g_attn, w_in_bf, cos_tab, sin_tab, g_v, *, tm, seq_tiles, q_dtype, q_scale, emit_kmean):
    m, d = x2d.shape
    n_tiles = m // tm
    n_seq = n_tiles // seq_tiles
    keep = min(SGU_CHUNK, tm)
    row = lambda w: pl.BlockSpec((tm, w), lambda r: (r, 0))
    tab = pl.BlockSpec((tm, ATTN_W), lambda r: (r % seq_tiles, 0))
    chan = pl.BlockSpec((1, ATTN_W, tm), lambda r: (r // seq_tiles, 0, r % seq_tiles))
    chan_shape = lambda dt: jax.ShapeDtypeStruct((n_seq, ATTN_W, seq_tiles * tm), dt)
    out_shape = [
        chan_shape(q_dtype),
        chan_shape(F32),
        chan_shape(F32),
        jax.ShapeDtypeStruct((m, ATTN_W), BF16),
        chan_shape(BF16),
        jax.ShapeDtypeStruct((m, SGU_W), BF16),
        jax.ShapeDtypeStruct((m, SGU_W), BF16),
        jax.ShapeDtypeStruct((m, d), BF16),
        jax.ShapeDtypeStruct((m, d), BF16),
        jax.ShapeDtypeStruct((n_seq * keep, SGU_W), F32),
    ]
    out_specs = [chan, chan, chan, row(ATTN_W), chan] + [row(SGU_W)] * 2 + [row(d)] * 2 + [
        pl.BlockSpec((keep, SGU_W), lambda r: (r // seq_tiles, 0))]
    if emit_kmean:
        assert tm % MOBA_BLOCK == 0
        out_shape.append(jax.ShapeDtypeStruct((m // MOBA_BLOCK, 1, ATTN_W), F32))
        out_specs.append(pl.BlockSpec((tm // MOBA_BLOCK, 1, ATTN_W), lambda r: (r, 0, 0)))
    return pl.pallas_call(
        functools.partial(_inproj_kernel, q_scale=q_scale),
        grid=(n_tiles,),
        in_specs=[row(d), _const_spec((1, d)), _const_spec(w_in_bf.shape), tab, tab, _const_spec((1, SGU_W))],
        out_specs=out_specs,
        out_shape=out_shape,
        scratch_shapes=[pltpu.VMEM((tm, ATTN_W), F32)],
        compiler_params=pltpu.CompilerParams(dimension_semantics=("arbitrary",), vmem_limit_bytes=VMEM_LIMIT),
        name="inproj",
    )(x2d, g_attn.reshape(1, d), w_in_bf, cos_tab, sin_tab, g_v.reshape(1, SGU_W))


def _moba_bias(gate, tile, n_blocks):
    blk = lax.broadcasted_iota(jnp.int32, gate.shape, 0).astype(F32)
    neg = NEG * PROMPT_Q_SCALE
    gate = jnp.where(blk < tile, gate, jnp.where(blk < n_blocks, neg, -jnp.inf))
    if tile <= MOBA_TOPK:
        return jnp.where(gate >= neg, 0.0, jnp.where(blk < tile, NEG, 0.0))
    cand = gate
    for t in range(MOBA_TOPK):
        best = jnp.max(cand, axis=0, keepdims=True)
        idx = jnp.min(jnp.where(cand == best, blk, float(gate.shape[0])), axis=0, keepdims=True)
        if t + 1 < MOBA_TOPK:
            cand = jnp.where(blk == idx, -jnp.inf, cand)
    bias = jnp.where(gate > best, 0.0, jnp.where(gate == best, jnp.where(blk <= idx, 0.0, NEG), NEG))
    return jnp.where(blk >= tile, 0.0, bias)


def _attn_tile(q_ref, k_ref, v_ref, km_ref, o_ref, s_sc, tile):
    tq = MOBA_BLOCK
    g_n = HEADS_PER_LANE_BLOCK
    cols = g_n * tq
    n_blocks = km_ref.shape[1]
    q_t = q_ref[0]
    chan = lax.broadcasted_iota(jnp.int32, (LANES, tq), 0)
    qs = jnp.concatenate([jnp.where(chan // HEAD_DIM == g, q_t, jnp.zeros_like(q_t)) for g in range(g_n)], axis=1)

    bias = None
    if tile > 0:
        pad_rows = -n_blocks % BF16_SUBLANES
        km = jnp.concatenate([km_ref[0], jnp.zeros((pad_rows, LANES), F32)], axis=0)
        km_hi = km.astype(BF16)
        km_lo = (km - km_hi.astype(F32)).astype(BF16)
        gate = (jnp.dot(km_hi, qs, preferred_element_type=F32) + jnp.dot(km_lo, qs, preferred_element_type=F32))
        bias = _moba_bias(gate, tile, n_blocks)

    s_all = jnp.dot(k_ref[0, 0:(tile + 1) * MOBA_BLOCK, :], qs, preferred_element_type=F32)
    kpos = lax.broadcasted_iota(jnp.int32, (MOBA_BLOCK, cols), 0)
    qpos = lax.broadcasted_iota(jnp.int32, (MOBA_BLOCK, cols), 1) & (tq - 1)
    groups = MOBA_BLOCK // F32_SUBLANES
    run_max = None
    for j in range(tile + 1):
        s = s_all[j * MOBA_BLOCK:(j + 1) * MOBA_BLOCK]
        if j == tile:
            s = jnp.where(kpos <= qpos, s, NEG)
        s_sc[j] = s
        part = jnp.max(s.reshape(groups, F32_SUBLANES, cols), axis=0)
        if j < tile:
            part = part + bias[j:j + 1, :]
        run_max = part if run_max is None else jnp.maximum(run_max, part)
    m = jnp.max(run_max, axis=0, keepdims=True)

    acc = None
    run_sum = None
    for j in range(tile + 1):
        shift = m - bias[j:j + 1, :] if j < tile else m
        p = jnp.exp2(s_sc[j] - shift)
        part = jnp.sum(p.reshape(groups, F32_SUBLANES, cols), axis=0)
        run_sum = part if run_sum is None else run_sum + part
        pv = jnp.dot(v_ref[0, :, j * MOBA_BLOCK:(j + 1) * MOBA_BLOCK], p.astype(BF16), preferred_element_type=F32)
        acc = pv if acc is None else acc + pv
    out = acc / jnp.sum(run_sum, axis=0, keepdims=True)
    o_t = jnp.zeros((LANES, tq), F32)
    for g in range(g_n):
        o_t = jnp.where(chan // HEAD_DIM == g, out[:, g * tq:(g + 1) * tq], o_t)
    o_ref[0] = o_t.T.astype(o_ref.dtype)


def _prompt_attn_kernel(q_ref, k_ref, v_ref, km_ref, o_ref, s_sc, *, n_blocks):
    i = pl.program_id(2)
    for tile in range(n_blocks):
        @pl.when(i == tile)
        def _(tile=tile):
            _attn_tile(q_ref, k_ref, v_ref, km_ref, o_ref, s_sc, tile)


def _prompt_attention(q_t, k, v_t, kmean):
    b, l, _ = k.shape
    n_blocks = l // MOBA_BLOCK
    assert n_blocks <= LANES
    cols = HEADS_PER_LANE_BLOCK * MOBA_BLOCK
    return pl.pallas_call(
        functools.partial(_prompt_attn_kernel, n_blocks=n_blocks),
        grid=(b, ATTN_W // LANES, n_blocks),
        in_specs=[
            pl.BlockSpec((1, LANES, MOBA_BLOCK), lambda bi, c, i: (bi, c, i)),
            pl.BlockSpec((1, l, LANES), lambda bi, c, i: (bi, 0, c)),
            pl.BlockSpec((1, LANES, l), lambda bi, c, i: (bi, c, 0)),
            pl.BlockSpec((1, n_blocks, LANES), lambda bi, c, i: (bi, 0, c)),
        ],
        out_specs=pl.BlockSpec((1, MOBA_BLOCK, LANES), lambda bi, c, i: (bi, i, c)),
        out_shape=jax.ShapeDtypeStruct((b, l, ATTN_W), BF16),
        scratch_shapes=[pltpu.VMEM((n_blocks, MOBA_BLOCK, cols), F32)],
        compiler_params=pltpu.CompilerParams(
            dimension_semantics=("arbitrary", "arbitrary", "arbitrary"), vmem_limit_bytes=VMEM_LIMIT),
        name="prompt_attn",
    )(q_t, k, v_t, kmean)


def _column(ref, seq):
    a = ref[...]
    seq_id = lax.broadcasted_iota(jnp.int32, a.shape, 1)
    return jnp.sum(jnp.where(seq_id == seq, a, 0.0), axis=1, keepdims=True)


def _page_scores(kt, q_bcast):
    return jnp.sum((kt * q_bcast).reshape(N_HEADS, HEAD_DIM, kt.shape[1]), axis=1)


def _score_cached_keys(step, pt_ref, qt_ref, ck_hbm, sc_ref, kbuf, qb_sc, sem_k):
    seqs_per_step = sc_ref.shape[0]
    n_seq = pl.num_programs(0) * seqs_per_step
    cps, chunk_pages = kbuf.shape[0], kbuf.shape[1]

    def k_copy(seq, c, p):
        return pltpu.make_async_copy(ck_hbm.at[pt_ref[seq, c * chunk_pages + p]], kbuf.at[c, p], sem_k.at[c])

    def start_chunk(seq, c):
        for p in range(chunk_pages):
            k_copy(seq, c, p).start(priority=p % DMA_PRIORITIES)

    @pl.when(step == 0)
    def _():
        for c in range(cps):
            start_chunk(0, c)

    for i in range(seqs_per_step):
        seq = step * seqs_per_step + i
        qb_sc[...] = jnp.broadcast_to(_column(qt_ref, seq), qb_sc.shape)
        for c in range(cps):
            for p in range(chunk_pages):
                k_copy(seq, c, p).wait()

            def score_body(p, carry, i=i, c=c):
                sc_ref[i, c * chunk_pages + p] = _page_scores(kbuf[c, p], qb_sc[...])
                return carry
            lax.fori_loop(0, chunk_pages, score_body, 0, unroll=2)

            @pl.when(seq + 1 < n_seq)
            def _(seq=seq, c=c):
                start_chunk(seq + 1, c)


def _sample_attn_kernel(pt_ref, sc_ref, q_ref, kn_ref, vn_ref, cv_hbm, o_ref,
                        vbuf, s_sc, sem_v, *, n_pages, page):
    b = pl.program_id(0)
    ppb = MOBA_BLOCK // page
    n_past = n_pages // ppb

    @pl.when(b == 0)
    def _():
        o_ref[...] = jnp.zeros_like(o_ref)

    q_col, kn_col, vn_col = _column(q_ref, b), _column(kn_ref, b), _column(vn_ref, b)

    pos_w = lax.broadcasted_iota(jnp.int32, (ATTN_W, page), 1)
    s_self = _page_scores(jnp.where(pos_w == 0, kn_col, 0.0), q_col)

    hl = lax.broadcasted_iota(jnp.int32, (N_HEADS, LANES), 1)
    cand = jnp.where(hl == n_past, NEG * SCALE, -jnp.inf).astype(F32)
    for j in range(n_past):
        blk = sc_ref[0, j * ppb]
        for r in range(1, ppb):
            blk = blk + sc_ref[0, j * ppb + r]
        cand = jnp.where(hl == j, jnp.sum(blk, axis=1, keepdims=True) * (1.0 / MOBA_BLOCK), cand)
    rank = jnp.zeros((N_HEADS, LANES), F32)
    for j in range(n_past + 1):
        gj = cand[:, j:j + 1]
        rank = rank + jnp.where(gj > cand, 1.0, jnp.where(gj == cand, jnp.where(hl > j, 1.0, 0.0), 0.0))
    hl_f = hl.astype(F32)
    picks = [jnp.sum(jnp.where(rank == t, hl_f, 0.0), axis=1, keepdims=True).astype(jnp.int32)
             for t in range(MOBA_TOPK)]

    blocks = [[picks[t][h, 0] for t in range(MOBA_TOPK)] for h in range(N_HEADS)]
    clamped = [[jnp.minimum(blk, n_past - 1) for blk in row] for row in blocks]

    def v_copy(h, t, r):
        phys = pt_ref[b, clamped[h][t] * ppb + r]
        return pltpu.make_async_copy(cv_hbm.at[phys, pl.ds(h * HEAD_DIM, HEAD_DIM), :],
                                     vbuf.at[(h * MOBA_TOPK + t) * ppb + r], sem_v)

    for h in range(N_HEADS):
        for t in range(MOBA_TOPK):
            for r in range(ppb):
                v_copy(h, t, r).start(priority=(h * MOBA_TOPK + t) % DMA_PRIORITIES)

    s_all = sc_ref[0]
    blk_id = lax.broadcasted_iota(jnp.int32, s_all.shape, 0) // ppb
    chosen = blk_id == picks[0][None]
    for t in range(1, MOBA_TOPK):
        chosen = chosen | (blk_id == picks[t][None])
    s_all = jnp.where(chosen, s_all, NEG)
    s_self = jnp.where(hl == 0, s_self, NEG)
    m = jnp.max(jnp.maximum(jnp.max(s_all, axis=0), s_self), axis=1, keepdims=True)
    p_all = jnp.exp(s_all - m[None])
    p_self = jnp.exp(s_self - m)
    denom = jnp.sum(jnp.sum(p_all, axis=0) + p_self, axis=1, keepdims=True)
    s_sc[...] = p_all / denom[None]
    p_self = p_self / denom

    for h in range(N_HEADS):
        for t in range(MOBA_TOPK):
            for r in range(ppb):
                v_copy(h, t, r).wait()

    pos_h = lax.broadcasted_iota(jnp.int32, (HEAD_DIM, page), 1)
    cols = []
    for h in range(N_HEADS):
        rows = slice(h * HEAD_DIM, (h + 1) * HEAD_DIM)
        acc = p_self[h:h + 1, :] * jnp.where(pos_h == 0, vn_col[rows], 0.0)
        for t in range(MOBA_TOPK):
            for r in range(ppb):
                p_row = s_sc[clamped[h][t] * ppb + r, pl.ds(h, 1), :]
                p_row = jnp.where(blocks[h][t] < n_past, p_row, 0.0)
                acc = acc + p_row * vbuf[(h * MOBA_TOPK + t) * ppb + r]
        cols.append(jnp.sum(acc, axis=1, keepdims=True))
    o_col = jnp.concatenate(cols, axis=0)
    seq_id = lax.broadcasted_iota(jnp.int32, o_ref.shape, 1)
    o_ref[...] = jnp.where(seq_id == b, o_col, o_ref[...])


def _channel_major_pages(cache):
    n_phys, page = cache.shape[0], cache.shape[1]
    return cache.transpose(0, 2, 3, 1).reshape(n_phys, ATTN_W, page)


def _sample_attention(scores, q_t, k_new_t, v_new_t, cache_v, page_table):
    s, n_pages = page_table.shape
    page = cache_v.shape[1]
    assert page == LANES and MOBA_BLOCK % page == 0 and (n_pages * page) % MOBA_BLOCK == 0
    n_past = n_pages * page // MOBA_BLOCK
    assert 1 <= n_past < LANES and s <= LANES
    n_vbuf = N_HEADS * MOBA_TOPK * (MOBA_BLOCK // page)
    tok = pl.BlockSpec((ATTN_W, s), lambda b, pt: (0, 0))
    grid_spec = pltpu.PrefetchScalarGridSpec(
        num_scalar_prefetch=1,
        grid=(s,),
        in_specs=[pl.BlockSpec((1, n_pages, N_HEADS, page), lambda b, pt: (b, 0, 0, 0)), tok, tok, tok,
                  pl.BlockSpec(memory_space=pl.ANY)],
        out_specs=tok,
        scratch_shapes=[
            pltpu.VMEM((n_vbuf, HEAD_DIM, page), F32),
            pltpu.VMEM((n_pages, N_HEADS, page), F32),
            pltpu.SemaphoreType.DMA(()),
        ],
    )
    return pl.pallas_call(
        functools.partial(_sample_attn_kernel, n_pages=n_pages, page=page),
        grid_spec=grid_spec,
        out_shape=jax.ShapeDtypeStruct((ATTN_W, s), F32),
        compiler_params=pltpu.CompilerParams(dimension_semantics=("arbitrary",), vmem_limit_bytes=VMEM_LIMIT),
        name="sample_attn",
    )(page_table, scores, q_t, k_new_t, v_new_t, _channel_major_pages(cache_v))


def _merge_kernel(x_ref, oa_ref, u_ref, vv_ref, sga_ref, sgb_ref, ws_ref, bs_ref, wpa_ref, wpb_ref, wo_ref,
                  x1_ref, *, single_position):
    tm = x_ref.shape[0]
    if single_position:
        s = vv_ref[...].astype(F32) * ws_ref[...] + bs_ref[...]
    else:
        t_idx = lax.broadcasted_iota(jnp.int32, (SGU_CHUNK, SGU_CHUNK), 0)
        s_idx = lax.broadcasted_iota(jnp.int32, (SGU_CHUNK, SGU_CHUNK), 1)
        ws = [jnp.where(s_idx <= t_idx, ws_ref[g], 0.0).astype(BF16) for g in range(SGU_GROUPS)]
        lane = lax.broadcasted_iota(jnp.int32, (SGU_CHUNK, LANES), 1)
        groups_per_block = LANES // SGU_GROUP_W
        chunks = []
        for c in range(tm // SGU_CHUNK):
            parts = []
            for blk in range(SGU_W // LANES):
                vp = vv_ref[c * SGU_CHUNK:(c + 1) * SGU_CHUNK, blk * LANES:(blk + 1) * LANES]
                sp = jnp.zeros((SGU_CHUNK, LANES), F32)
                for g in range(groups_per_block):
                    sg = jnp.dot(ws[blk * groups_per_block + g], vp, preferred_element_type=F32)
                    sp = jnp.where(lane // SGU_GROUP_W == g, sg, sp)
                parts.append(sp)
            chunks.append(jnp.concatenate(parts, axis=1) + bs_ref[...])
        s = jnp.concatenate(chunks, axis=0)
    ob = (u_ref[...].astype(F32) * s).astype(BF16)
    ba = jnp.dot(oa_ref[...], wpa_ref[...], preferred_element_type=F32)
    bb = jnp.dot(ob, wpb_ref[...], preferred_element_type=F32)
    merged = (sga_ref[...].astype(F32) * ba + sgb_ref[...].astype(F32) * bb).astype(BF16)
    x1_ref[...] = x_ref[...] + jnp.dot(merged, wo_ref[...], preferred_element_type=F32)


def _merge(x2d, o_a, u, vv, sga, sgb, w_s, b_s, w_pa_bf, w_pb_bf, w_o_bf, *, tm, single_position):
    m, d = x2d.shape
    if single_position:
        ws_in = jnp.repeat(w_s[:, 0, 0], SGU_GROUP_W).reshape(1, SGU_W)
        bs_in = jnp.repeat(b_s[:, 0], SGU_GROUP_W).reshape(1, SGU_W)
    else:
        ws_in = w_s
        bs_in = jnp.repeat(b_s.T, SGU_GROUP_W, axis=1)
    row = lambda w: pl.BlockSpec((tm, w), lambda r: (r, 0))
    return pl.pallas_call(
        functools.partial(_merge_kernel, single_position=single_position),
        grid=(m // tm,),
        in_specs=[row(d), row(ATTN_W), row(SGU_W), row(SGU_W), row(d), row(d),
                  _const_spec(ws_in.shape), _const_spec(bs_in.shape),
                  _const_spec(w_pa_bf.shape), _const_spec(w_pb_bf.shape), _const_spec(w_o_bf.shape)],
        out_specs=row(d),
        out_shape=jax.ShapeDtypeStruct((m, d), F32),
        compiler_params=pltpu.CompilerParams(dimension_semantics=("arbitrary",), vmem_limit_bytes=VMEM_LIMIT),
        name="merge",
    )(x2d, o_a, u, vv, sga, sgb, ws_in, bs_in, w_pa_bf, w_pb_bf, w_o_bf)


FFN_CHUNK = 1024


def _ffn_kernel(x_ref, g_ref, wup_ref, wdn_ref, gf_ref, y_ref):
    x = x_ref[...]
    h = _rmsnorm(x, g_ref[...]).astype(BF16)
    acc = x
    for c in range(wup_ref.shape[1] // FFN_CHUNK):
        a = jnp.dot(h, wup_ref[:, c * FFN_CHUNK:(c + 1) * FFN_CHUNK], preferred_element_type=F32)
        a = jnp.square(jnp.maximum(a, 0.0)).astype(BF16)
        acc = acc + jnp.dot(a, wdn_ref[c * FFN_CHUNK:(c + 1) * FFN_CHUNK, :], preferred_element_type=F32)
    y_ref[...] = _rmsnorm(acc, gf_ref[...])


def _ffn(x2d, g_ffn, w_up_bf, w_down_bf, g_final, *, tm):
    m, d = x2d.shape
    row = pl.BlockSpec((tm, d), lambda r: (r, 0))
    return pl.pallas_call(
        _ffn_kernel,
        grid=(m // tm,),
        in_specs=[row, _const_spec((1, d)), _const_spec(w_up_bf.shape), _const_spec(w_down_bf.shape),
                  _const_spec((1, d))],
        out_specs=row,
        out_shape=jax.ShapeDtypeStruct((m, d), F32),
        compiler_params=pltpu.CompilerParams(dimension_semantics=("arbitrary",), vmem_limit_bytes=VMEM_LIMIT),
        name="ffn",
    )(x2d, g_ffn.reshape(1, d), w_up_bf, w_down_bf, g_final.reshape(1, d))


def _ffn_score_kernel(pt_ref, x_ref, g_ref, wup_ref, wdn_ref, gf_ref, qt_ref, ck_hbm, y_ref, sc_ref,
                      kbuf, qb_sc, sem_k):
    _score_cached_keys(pl.program_id(0), pt_ref, qt_ref, ck_hbm, sc_ref, kbuf, qb_sc, sem_k)
    _ffn_kernel(x_ref, g_ref, wup_ref, wdn_ref, gf_ref, y_ref)


def _ffn_and_sample_scores(x2d, g_ffn, w_up_bf, w_down_bf, g_final, q_t, cache_k, page_table, *, tm):
    m, d = x2d.shape
    s, n_pages = page_table.shape
    page = cache_k.shape[1]
    n_steps = m // tm
    assert s % n_steps == 0 and s <= LANES
    chunk_pages = min(SAMPLE_CHUNK_PAGES, n_pages)
    assert n_pages % chunk_pages == 0
    row = pl.BlockSpec((tm, d), lambda r, pt: (r, 0))
    const = lambda shape: pl.BlockSpec(shape, lambda r, pt: (0,) * len(shape), pipeline_mode=pl.Buffered(1))
    grid_spec = pltpu.PrefetchScalarGridSpec(
        num_scalar_prefetch=1,
        grid=(n_steps,),
        in_specs=[row, const((1, d)), const(w_up_bf.shape), const(w_down_bf.shape), const((1, d)),
                  const((ATTN_W, s)), pl.BlockSpec(memory_space=pl.ANY)],
        out_specs=[row, pl.BlockSpec((s // n_steps, n_pages, N_HEADS, page), lambda r, pt: (r, 0, 0, 0))],
        scratch_shapes=[
            pltpu.VMEM((n_pages // chunk_pages, chunk_pages, ATTN_W, page), F32),
            pltpu.VMEM((ATTN_W, page), F32),
            pltpu.SemaphoreType.DMA((n_pages // chunk_pages,)),
        ],
    )
    return pl.pallas_call(
        _ffn_score_kernel,
        grid_spec=grid_spec,
        out_shape=[jax.ShapeDtypeStruct((m, d), F32), jax.ShapeDtypeStruct((s, n_pages, N_HEADS, page), F32)],
        compiler_params=pltpu.CompilerParams(dimension_semantics=("arbitrary",), vmem_limit_bytes=VMEM_LIMIT),
        name="ffn_scores",
    )(page_table, x2d, g_ffn.reshape(1, d), w_up_bf, w_down_bf, g_final.reshape(1, d), q_t,
      _channel_major_pages(cache_k))


def _rope_tables(pos):
    half = HEAD_DIM // 2
    inv = ROPE_THETA ** (-2.0 * jnp.arange(half, dtype=F32) / HEAD_DIM)
    ang = pos.astype(F32)[:, None] * inv[None, :]
    cos = jnp.tile(jnp.cos(ang), (1, 2 * N_HEADS))
    sin = jnp.tile(jnp.concatenate([-jnp.sin(ang), jnp.sin(ang)], axis=1), (1, N_HEADS))
    return cos, sin


def kernel(x_prompt, x_sample, cache_k, cache_v, page_table, g_attn, w_in, w_pa, w_pb, w_o, g_v, w_s, b_s,
           g_ffn, w_up, w_down, g_final):
    depth = w_in.shape[0]
    assert depth == 1, "the final norm is fused into the last layer's MLP kernel"
    bsz, seq, d = x_prompt.shape
    n_dec, dec_seq, _ = x_sample.shape
    assert dec_seq == 1 and seq % MOBA_BLOCK == 0
    past_len = page_table.shape[1] * cache_k.shape[2]
    n_chunk_rows = seq - ((seq - 1) // SGU_CHUNK) * SGU_CHUNK
    assert n_chunk_rows == SGU_CHUNK

    cos_p, sin_p = _rope_tables(jnp.arange(seq, dtype=jnp.int32))
    cos_s, sin_s = _rope_tables(jnp.full((n_dec,), past_len, dtype=jnp.int32))

    xp = x_prompt.reshape(bsz * seq, d)
    xs = x_sample.reshape(n_dec, d)
    l = 0
    w_in_bf, w_pa_bf, w_pb_bf, w_o_bf = (w[l].astype(BF16) for w in (w_in, w_pa, w_pb, w_o))
    w_up_bf, w_down_bf = w_up[l].astype(BF16), w_down[l].astype(BF16)

    tm_p = PROMPT_ROW_TILE
    assert seq % tm_p == 0
    (q_t, k_p, v_p, kb, vb_t, u, vv, sga, sgb, sguv_p, kmean) = _inproj(
        xp, g_attn[l], w_in_bf, cos_p, sin_p, g_v[l], tm=MOBA_BLOCK, seq_tiles=seq // MOBA_BLOCK, q_dtype=BF16,
        q_scale=PROMPT_Q_SCALE, emit_kmean=True)
    o_a = _prompt_attention(q_t, kb.reshape(bsz, seq, ATTN_W), vb_t,
                            kmean.reshape(bsz, seq // MOBA_BLOCK, ATTN_W)).reshape(bsz * seq, ATTN_W)
    x1 = _merge(xp, o_a, u, vv, sga, sgb, w_s[l], b_s[l], w_pa_bf, w_pb_bf, w_o_bf, tm=tm_p, single_position=False)

    (q_s, k_s, v_s, _, _, u_s, vv_s, sga_s, sgb_s, sguv_s) = _inproj(
        xs, g_attn[l], w_in_bf, cos_s, sin_s, g_v[l], tm=n_dec, seq_tiles=1, q_dtype=F32, q_scale=SCALE,
        emit_kmean=False)
    y_p, scores = _ffn_and_sample_scores(x1, g_ffn[l], w_up_bf, w_down_bf, g_final, q_s[0], cache_k[l], page_table,
                                         tm=tm_p)
    o_as = _sample_attention(scores, q_s[0], k_s[0], v_s[0], cache_v[l], page_table).T.astype(BF16)
    x1_s = _merge(xs, o_as, u_s, vv_s, sga_s, sgb_s, w_s[l], b_s[l], w_pa_bf, w_pb_bf, w_o_bf,
                  tm=n_dec, single_position=True)
    y_s = _ffn(x1_s, g_ffn[l], w_up_bf, w_down_bf, g_final, tm=n_dec)

    def heads_last(t, n, length):
        return t.reshape(1, n, N_HEADS, HEAD_DIM, length).transpose(0, 1, 4, 2, 3)

    return (y_p.reshape(bsz, seq, d),
            y_s.reshape(n_dec, 1, d),
            heads_last(k_p, bsz, seq),
            heads_last(v_p, bsz, seq),
            sguv_p.reshape(1, bsz, SGU_CHUNK, SGU_W),
            heads_last(k_s, 1, n_dec).reshape(1, n_dec, 1, N_HEADS, HEAD_DIM),
            heads_last(v_s, 1, n_dec).reshape(1, n_dec, 1, N_HEADS, HEAD_DIM),
            sguv_s.reshape(1, n_dec, 1, SGU_W))
```

```python
import functools

import jax
import jax.numpy as jnp
from jax import lax
from jax.experimental import pallas as pl
from jax.experimental.pallas import tpu as pltpu

N_HEADS = 8
HEAD_DIM = 64
ATTN_W = N_HEADS * HEAD_DIM
MOBA_BLOCK = 256
MOBA_TOPK = 3
SGU_GROUPS = 8
SGU_W = 512
SGU_GROUP_W = SGU_W // SGU_GROUPS
SGU_CHUNK = 128
ROPE_THETA = 10000.0
EPS = 1e-6
NEG = -1e30
SCALE = HEAD_DIM ** -0.5
LOG2E = 1.4426950408889634
PROMPT_Q_SCALE = SCALE * LOG2E

LANES = 128
F32_SUBLANES = 8
BF16_SUBLANES = 16
HEADS_PER_LANE_BLOCK = LANES // HEAD_DIM
VMEM_LIMIT = 56 * 1024 * 1024
PROMPT_ROW_TILE = 2 * MOBA_BLOCK
SAMPLE_CHUNK_PAGES = 16
DMA_PRIORITIES = 2

F32 = jnp.float32
BF16 = jnp.bfloat16
NT_DIMS = (((1,), (1,)), ((), ()))


def _rmsnorm(x, g):
    return x * lax.rsqrt(jnp.mean(x * x, axis=-1, keepdims=True) + EPS) * g


def _const_spec(shape):
    return pl.BlockSpec(shape, lambda *_: (0,) * len(shape), pipeline_mode=pl.Buffered(1))


def _rope(t, cos, sin_signed):
    lane = lax.broadcasted_iota(jnp.int32, (t.shape[0], LANES), 1)
    first_half = (lane & (HEAD_DIM - 1)) < HEAD_DIM // 2
    outs = []
    for c in range(t.shape[1] // LANES):
        sl = slice(c * LANES, (c + 1) * LANES)
        tc = t[:, sl]
        partner = jnp.where(first_half, pltpu.roll(tc, LANES - HEAD_DIM // 2, 1),
                            pltpu.roll(tc, HEAD_DIM // 2, 1))
        outs.append(tc * cos[:, sl] + partner * sin_signed[:, sl])
    return jnp.concatenate(outs, axis=1)


def _inproj_kernel(x_ref, g_ref, w_ref, cos_ref, sin_ref, gv_ref,
                   q_ref, kf_ref, vf_ref, kb_ref, vb_ref, u_ref, vv_ref, sga_ref, sgb_ref, sguv_ref,
                   *maybe_kmean_and_scratch, q_scale):
    *maybe_kmean_ref, v_sc = maybe_kmean_and_scratch
    d_model = x_ref.shape[1]
    h = _rmsnorm(x_ref[...], g_ref[...]).astype(BF16)

    def proj(lo, width):
        return jnp.dot(h, w_ref[:, lo:lo + width], preferred_element_type=F32)

    cos = cos_ref[...]
    sin = sin_ref[...]
    q = _rope(proj(0, ATTN_W), cos, sin) * q_scale
    q_ref[0] = q.T.astype(q_ref.dtype)
    k = _rope(proj(ATTN_W, ATTN_W), cos, sin)
    kf_ref[0] = k.T
    kb_ref[...] = k.astype(BF16)
    if maybe_kmean_ref:
        blocks = k.shape[0] // MOBA_BLOCK
        maybe_kmean_ref[0][:, 0, :] = jnp.mean(k.reshape(blocks, MOBA_BLOCK, ATTN_W), axis=1)
    v_sc[...] = proj(2 * ATTN_W, ATTN_W)
    v_t = v_sc[...].T
    vf_ref[0] = v_t
    vb_ref[0] = v_t.astype(BF16)
    base = 3 * ATTN_W
    u_ref[...] = jax.nn.gelu(proj(base, SGU_W)).astype(BF16)
    vv = _rmsnorm(jax.nn.gelu(proj(base + SGU_W, SGU_W)), gv_ref[...])
    vv_ref[...] = vv.astype(BF16)
    sguv_ref[...] = vv[vv.shape[0] - sguv_ref.shape[0]:, :]
    base += 2 * SGU_W
    sga_ref[...] = jax.nn.sigmoid(proj(base, d_model)).astype(BF16)
    sgb_ref[...] = jax.nn.sigmoid(proj(base + d_model, d_model)).astype(BF16)


def _inproj(x2d, g_attn, w_in_bf, cos_tab, sin_tab, g_v, *, tm, seq_tiles, q_dtype, q_scale, emit_kmean):
    m, d = x2d.shape
    n_tiles = m // tm
    n_seq = n_tiles // seq_tiles
    keep = min(SGU_CHUNK, tm)
    row = lambda w: pl.BlockSpec((tm, w), lambda r: (r, 0))
    tab = pl.BlockSpec((tm, ATTN_W), lambda r: (r % seq_tiles, 0))
    chan = pl.BlockSpec((1, ATTN_W, tm), lambda r: (r // seq_tiles, 0, r % seq_tiles))
    chan_shape = lambda dt: jax.ShapeDtypeStruct((n_seq, ATTN_W, seq_tiles * tm), dt)
    out_shape = [
        chan_shape(q_dtype),
        chan_shape(F32),
        chan_shape(F32),
        jax.ShapeDtypeStruct((m, ATTN_W), BF16),
        chan_shape(BF16),
        jax.ShapeDtypeStruct((m, SGU_W), BF16),
        jax.ShapeDtypeStruct((m, SGU_W), BF16),
        jax.ShapeDtypeStruct((m, d), BF16),
        jax.ShapeDtypeStruct((m, d), BF16),
        jax.ShapeDtypeStruct((n_seq * keep, SGU_W), F32),
    ]
    out_specs = [chan, chan, chan, row(ATTN_W), chan] + [row(SGU_W)] * 2 + [row(d)] * 2 + [
        pl.BlockSpec((keep, SGU_W), lambda r: (r // seq_tiles, 0))]
    if emit_kmean:
        assert tm % MOBA_BLOCK == 0
        out_shape.append(jax.ShapeDtypeStruct((m // MOBA_BLOCK, 1, ATTN_W), F32))
        out_specs.append(pl.BlockSpec((tm // MOBA_BLOCK, 1, ATTN_W), lambda r: (r, 0, 0)))
    return pl.pallas_call(
        functools.partial(_inproj_kernel, q_scale=q_scale),
        grid=(n_tiles,),
        in_specs=[row(d), _const_spec((1, d)), _const_spec(w_in_bf.shape), tab, tab, _const_spec((1, SGU_W))],
        out_specs=out_specs,
        out_shape=out_shape,
        scratch_shapes=[pltpu.VMEM((tm, ATTN_W), F32)],
        compiler_params=pltpu.CompilerParams(dimension_semantics=("arbitrary",), vmem_limit_bytes=VMEM_LIMIT),
        name="inproj",
    )(x2d, g_attn.reshape(1, d), w_in_bf, cos_tab, sin_tab, g_v.reshape(1, SGU_W))


def _moba_bias(gate, tile, n_blocks):
    blk = lax.broadcasted_iota(jnp.int32, gate.shape, 0).astype(F32)
    neg = NEG * PROMPT_Q_SCALE
    gate = jnp.where(blk < tile, gate, jnp.where(blk < n_blocks, neg, -jnp.inf))
    if tile <= MOBA_TOPK:
        return jnp.where(gate >= neg, 0.0, jnp.where(blk < tile, NEG, 0.0))
    cand = gate
    for t in range(MOBA_TOPK):
        best = jnp.max(cand, axis=0, keepdims=True)
        idx = jnp.min(jnp.where(cand == best, blk, float(gate.shape[0])), axis=0, keepdims=True)
        if t + 1 < MOBA_TOPK:
            cand = jnp.where(blk == idx, -jnp.inf, cand)
    bias = jnp.where(gate > best, 0.0, jnp.where(gate == best, jnp.where(blk <= idx, 0.0, NEG), NEG))
    return jnp.where(blk >= tile, 0.0, bias)


def _attn_tile(q_ref, k_ref, v_ref, km_ref, o_ref, s_sc, tile):
    tq = MOBA_BLOCK
    g_n = HEADS_PER_LANE_BLOCK
    cols = g_n * tq
    n_blocks = km_ref.shape[1]
    q_t = q_ref[0]
    chan = lax.broadcasted_iota(jnp.int32, (LANES, tq), 0)
    qs = jnp.concatenate([jnp.where(chan // HEAD_DIM == g, q_t, jnp.zeros_like(q_t)) for g in range(g_n)], axis=1)

    bias = None
    if tile > 0:
        pad_rows = -n_blocks % BF16_SUBLANES
        km = jnp.concatenate([km_ref[0], jnp.zeros((pad_rows, LANES), F32)], axis=0)
        km_hi = km.astype(BF16)
        km_lo = (km - km_hi.astype(F32)).astype(BF16)
        gate = (jnp.dot(km_hi, qs, preferred_element_type=F32) + jnp.dot(km_lo, qs, preferred_element_type=F32))
        bias = _moba_bias(gate, tile, n_blocks)

    s_all = jnp.dot(k_ref[0, 0:(tile + 1) * MOBA_BLOCK, :], qs, preferred_element_type=F32)
    kpos = lax.broadcasted_iota(jnp.int32, (MOBA_BLOCK, cols), 0)
    qpos = lax.broadcasted_iota(jnp.int32, (MOBA_BLOCK, cols), 1) & (tq - 1)
    groups = MOBA_BLOCK // F32_SUBLANES
    run_max = None
    for j in range(tile + 1):
        s = s_all[j * MOBA_BLOCK:(j + 1) * MOBA_BLOCK]
        if j == tile:
            s = jnp.where(kpos <= qpos, s, NEG)
        s_sc[j] = s
        part = jnp.max(s.reshape(groups, F32_SUBLANES, cols), axis=0)
        if j < tile:
            part = part + bias[j:j + 1, :]
        run_max = part if run_max is None else jnp.maximum(run_max, part)
    m = jnp.max(run_max, axis=0, keepdims=True)

    acc = None
    run_sum = None
    for j in range(tile + 1):
        shift = m - bias[j:j + 1, :] if j < tile else m
        p = jnp.exp2(s_sc[j] - shift)
        part = jnp.sum(p.reshape(groups, F32_SUBLANES, cols), axis=0)
        run_sum = part if run_sum is None else run_sum + part
        pv = jnp.dot(v_ref[0, :, j * MOBA_BLOCK:(j + 1) * MOBA_BLOCK], p.astype(BF16), preferred_element_type=F32)
        acc = pv if acc is None else acc + pv
    out = acc / jnp.sum(run_sum, axis=0, keepdims=True)
    o_t = jnp.zeros((LANES, tq), F32)
    for g in range(g_n):
        o_t = jnp.where(chan // HEAD_DIM == g, out[:, g * tq:(g + 1) * tq], o_t)
    o_ref[0] = o_t.T.astype(o_ref.dtype)


def _prompt_attn_kernel(q_ref, k_ref, v_ref, km_ref, o_ref, s_sc, *, n_blocks):
    i = pl.program_id(2)
    for tile in range(n_blocks):
        @pl.when(i == tile)
        def _(tile=tile):
            _attn_tile(q_ref, k_ref, v_ref, km_ref, o_ref, s_sc, tile)


def _prompt_attention(q_t, k, v_t, kmean):
    b, l, _ = k.shape
    n_blocks = l // MOBA_BLOCK
    assert n_blocks <= LANES
    cols = HEADS_PER_LANE_BLOCK * MOBA_BLOCK
    return pl.pallas_call(
        functools.partial(_prompt_attn_kernel, n_blocks=n_blocks),
        grid=(b, ATTN_W // LANES, n_blocks),
        in_specs=[
            pl.BlockSpec((1, LANES, MOBA_BLOCK), lambda bi, c, i: (bi, c, i)),
            pl.BlockSpec((1, l, LANES), lambda bi, c, i: (bi, 0, c)),
            pl.BlockSpec((1, LANES, l), lambda bi, c, i: (bi, c, 0)),
            pl.BlockSpec((1, n_blocks, LANES), lambda bi, c, i: (bi, 0, c)),
        ],
        out_specs=pl.BlockSpec((1, MOBA_BLOCK, LANES), lambda bi, c, i: (bi, i, c)),
        out_shape=jax.ShapeDtypeStruct((b, l, ATTN_W), BF16),
        scratch_shapes=[pltpu.VMEM((n_blocks, MOBA_BLOCK, cols), F32)],
        compiler_params=pltpu.CompilerParams(
            dimension_semantics=("arbitrary", "arbitrary", "arbitrary"), vmem_limit_bytes=VMEM_LIMIT),
        name="prompt_attn",
    )(q_t, k, v_t, kmean)


def _column(ref, seq):
    a = ref[...]
    seq_id = lax.broadcasted_iota(jnp.int32, a.shape, 1)
    return jnp.sum(jnp.where(seq_id == seq, a, 0.0), axis=1, keepdims=True)


def _page_scores(kt, q_bcast):
    return jnp.sum((kt * q_bcast).reshape(N_HEADS, HEAD_DIM, kt.shape[1]), axis=1)


def _score_cached_keys(step, pt_ref, qt_ref, ck_hbm, sc_ref, kbuf, qb_sc, sem_k, host_work):
    seqs_per_step = sc_ref.shape[0]
    n_seq = pl.num_programs(0) * seqs_per_step
    cps, chunk_pages = kbuf.shape[0], kbuf.shape[1]

    def k_copy(seq, c, p):
        return pltpu.make_async_copy(ck_hbm.at[pt_ref[seq, c * chunk_pages + p]], kbuf.at[c, p], sem_k.at[c])

    def start_chunk(seq, c):
        for p in range(chunk_pages):
            k_copy(seq, c, p).start(priority=p % DMA_PRIORITIES)

    @pl.when(step == 0)
    def _():
        for c in range(cps):
            start_chunk(0, c)

    for i in range(seqs_per_step):
        seq = step * seqs_per_step + i
        qb_sc[...] = jnp.broadcast_to(_column(qt_ref, seq), qb_sc.shape)
        for c in range(cps):
            for p in range(chunk_pages):
                k_copy(seq, c, p).wait()

            for p in range(chunk_pages):
                sc_ref[i, c * chunk_pages + p] = _page_scores(kbuf[c, p], qb_sc[...])
            host_work(i * cps + c)

            @pl.when(seq + 1 < n_seq)
            def _(seq=seq, c=c):
                start_chunk(seq + 1, c)


def _sample_attn_kernel(pt_ref, sc_ref, q_ref, kn_ref, vn_ref, cv_hbm, o_ref,
                        vbuf, s_sc, sem_v, *, n_pages, page):
    b = pl.program_id(0)
    ppb = MOBA_BLOCK // page
    n_past = n_pages // ppb

    @pl.when(b == 0)
    def _():
        o_ref[...] = jnp.zeros_like(o_ref)

    q_col, kn_col, vn_col = _column(q_ref, b), _column(kn_ref, b), _column(vn_ref, b)

    pos_w = lax.broadcasted_iota(jnp.int32, (ATTN_W, page), 1)
    s_self = _page_scores(jnp.where(pos_w == 0, kn_col, 0.0), q_col)

    hl = lax.broadcasted_iota(jnp.int32, (N_HEADS, LANES), 1)
    cand = jnp.where(hl == n_past, NEG * SCALE, -jnp.inf).astype(F32)
    for j in range(n_past):
        blk = sc_ref[0, j * ppb]
        for r in range(1, ppb):
            blk = blk + sc_ref[0, j * ppb + r]
        cand = jnp.where(hl == j, jnp.sum(blk, axis=1, keepdims=True) * (1.0 / MOBA_BLOCK), cand)
    rank = jnp.zeros((N_HEADS, LANES), F32)
    for j in range(n_past + 1):
        gj = cand[:, j:j + 1]
        rank = rank + jnp.where(gj > cand, 1.0, jnp.where(gj == cand, jnp.where(hl > j, 1.0, 0.0), 0.0))
    hl_f = hl.astype(F32)
    picks = [jnp.sum(jnp.where(rank == t, hl_f, 0.0), axis=1, keepdims=True).astype(jnp.int32)
             for t in range(MOBA_TOPK)]

    blocks = [[picks[t][h, 0] for t in range(MOBA_TOPK)] for h in range(N_HEADS)]
    clamped = [[jnp.minimum(blk, n_past - 1) for blk in row] for row in blocks]

    def v_copy(h, t, r):
        phys = pt_ref[b, clamped[h][t] * ppb + r]
        return pltpu.make_async_copy(cv_hbm.at[phys, pl.ds(h * HEAD_DIM, HEAD_DIM), :],
                                     vbuf.at[(h * MOBA_TOPK + t) * ppb + r], sem_v)

    for h in range(N_HEADS):
        for t in range(MOBA_TOPK):
            for r in range(ppb):
                v_copy(h, t, r).start(priority=(h * MOBA_TOPK + t) % DMA_PRIORITIES)

    s_all = sc_ref[0]
    blk_id = lax.broadcasted_iota(jnp.int32, s_all.shape, 0) // ppb
    chosen = blk_id == picks[0][None]
    for t in range(1, MOBA_TOPK):
        chosen = chosen | (blk_id == picks[t][None])
    s_all = jnp.where(chosen, s_all, NEG)
    s_self = jnp.where(hl == 0, s_self, NEG)
    m = jnp.max(jnp.maximum(jnp.max(s_all, axis=0), s_self), axis=1, keepdims=True)
    p_all = jnp.exp(s_all - m[None])
    p_self = jnp.exp(s_self - m)
    denom = jnp.sum(jnp.sum(p_all, axis=0) + p_self, axis=1, keepdims=True)
    s_sc[...] = p_all / denom[None]
    p_self = p_self / denom

    for h in range(N_HEADS):
        for t in range(MOBA_TOPK):
            for r in range(ppb):
                v_copy(h, t, r).wait()

    pos_h = lax.broadcasted_iota(jnp.int32, (HEAD_DIM, page), 1)
    cols = []
    for h in range(N_HEADS):
        rows = slice(h * HEAD_DIM, (h + 1) * HEAD_DIM)
        acc = p_self[h:h + 1, :] * jnp.where(pos_h == 0, vn_col[rows], 0.0)
        for t in range(MOBA_TOPK):
            for r in range(ppb):
                p_row = s_sc[clamped[h][t] * ppb + r, pl.ds(h, 1), :]
                p_row = jnp.where(blocks[h][t] < n_past, p_row, 0.0)
                acc = acc + p_row * vbuf[(h * MOBA_TOPK + t) * ppb + r]
        cols.append(jnp.sum(acc, axis=1, keepdims=True))
    o_col = jnp.concatenate(cols, axis=0)
    seq_id = lax.broadcasted_iota(jnp.int32, o_ref.shape, 1)
    o_ref[...] = jnp.where(seq_id == b, o_col, o_ref[...])


def _channel_major_pages(cache):
    n_phys, page = cache.shape[0], cache.shape[1]
    return cache.transpose(0, 2, 3, 1).reshape(n_phys, ATTN_W, page)


def _sample_attention(scores, q_t, k_new_t, v_new_t, cache_v, page_table):
    s, n_pages = page_table.shape
    page = cache_v.shape[1]
    assert page == LANES and MOBA_BLOCK % page == 0 and (n_pages * page) % MOBA_BLOCK == 0
    n_past = n_pages * page // MOBA_BLOCK
    assert 1 <= n_past < LANES and s <= LANES
    n_vbuf = N_HEADS * MOBA_TOPK * (MOBA_BLOCK // page)
    tok = pl.BlockSpec((ATTN_W, s), lambda b, pt: (0, 0))
    grid_spec = pltpu.PrefetchScalarGridSpec(
        num_scalar_prefetch=1,
        grid=(s,),
        in_specs=[pl.BlockSpec((1, n_pages, N_HEADS, page), lambda b, pt: (b, 0, 0, 0)), tok, tok, tok,
                  pl.BlockSpec(memory_space=pl.ANY)],
        out_specs=tok,
        scratch_shapes=[
            pltpu.VMEM((n_vbuf, HEAD_DIM, page), F32),
            pltpu.VMEM((n_pages, N_HEADS, page), F32),
            pltpu.SemaphoreType.DMA(()),
        ],
    )
    return pl.pallas_call(
        functools.partial(_sample_attn_kernel, n_pages=n_pages, page=page),
        grid_spec=grid_spec,
        out_shape=jax.ShapeDtypeStruct((ATTN_W, s), F32),
        compiler_params=pltpu.CompilerParams(dimension_semantics=("arbitrary",), vmem_limit_bytes=VMEM_LIMIT),
        name="sample_attn",
    )(page_table, scores, q_t, k_new_t, v_new_t, _channel_major_pages(cache_v))


def _merge_kernel(x_ref, oa_ref, u_ref, vv_ref, sga_ref, sgb_ref, ws_ref, bs_ref, wpa_ref, wpb_ref, wo_ref,
                  x1_ref, *, single_position):
    tm = x_ref.shape[0]
    if single_position:
        s = vv_ref[...].astype(F32) * ws_ref[...] + bs_ref[...]
    else:
        t_idx = lax.broadcasted_iota(jnp.int32, (SGU_CHUNK, SGU_CHUNK), 0)
        s_idx = lax.broadcasted_iota(jnp.int32, (SGU_CHUNK, SGU_CHUNK), 1)
        ws = [jnp.where(s_idx <= t_idx, ws_ref[g], 0.0).astype(BF16) for g in range(SGU_GROUPS)]
        lane = lax.broadcasted_iota(jnp.int32, (SGU_CHUNK, LANES), 1)
        groups_per_block = LANES // SGU_GROUP_W
        chunks = []
        for c in range(tm // SGU_CHUNK):
            parts = []
            for blk in range(SGU_W // LANES):
                vp = vv_ref[c * SGU_CHUNK:(c + 1) * SGU_CHUNK, blk * LANES:(blk + 1) * LANES]
                sp = jnp.zeros((SGU_CHUNK, LANES), F32)
                for g in range(groups_per_block):
                    sg = jnp.dot(ws[blk * groups_per_block + g], vp, preferred_element_type=F32)
                    sp = jnp.where(lane // SGU_GROUP_W == g, sg, sp)
                parts.append(sp)
            chunks.append(jnp.concatenate(parts, axis=1) + bs_ref[...])
        s = jnp.concatenate(chunks, axis=0)
    ob = (u_ref[...].astype(F32) * s).astype(BF16)
    ba = jnp.dot(oa_ref[...], wpa_ref[...], preferred_element_type=F32)
    bb = jnp.dot(ob, wpb_ref[...], preferred_element_type=F32)
    merged = (sga_ref[...].astype(F32) * ba + sgb_ref[...].astype(F32) * bb).astype(BF16)
    x1_ref[...] = x_ref[...] + jnp.dot(merged, wo_ref[...], preferred_element_type=F32)


def _merge(x2d, o_a, u, vv, sga, sgb, w_s, b_s, w_pa_bf, w_pb_bf, w_o_bf, *, tm, single_position):
    m, d = x2d.shape
    if single_position:
        ws_in = jnp.repeat(w_s[:, 0, 0], SGU_GROUP_W).reshape(1, SGU_W)
        bs_in = jnp.repeat(b_s[:, 0], SGU_GROUP_W).reshape(1, SGU_W)
    else:
        ws_in = w_s
        bs_in = jnp.repeat(b_s.T, SGU_GROUP_W, axis=1)
    row = lambda w: pl.BlockSpec((tm, w), lambda r: (r, 0))
    return pl.pallas_call(
        functools.partial(_merge_kernel, single_position=single_position),
        grid=(m // tm,),
        in_specs=[row(d), row(ATTN_W), row(SGU_W), row(SGU_W), row(d), row(d),
                  _const_spec(ws_in.shape), _const_spec(bs_in.shape),
                  _const_spec(w_pa_bf.shape), _const_spec(w_pb_bf.shape), _const_spec(w_o_bf.shape)],
        out_specs=row(d),
        out_shape=jax.ShapeDtypeStruct((m, d), F32),
        compiler_params=pltpu.CompilerParams(dimension_semantics=("arbitrary",), vmem_limit_bytes=VMEM_LIMIT),
        name="merge",
    )(x2d, o_a, u, vv, sga, sgb, ws_in, bs_in, w_pa_bf, w_pb_bf, w_o_bf)


FFN_CHUNK = 1024


def _ffn_kernel(x_ref, g_ref, wup_ref, wdn_ref, gf_ref, y_ref):
    x = x_ref[...]
    h = _rmsnorm(x, g_ref[...]).astype(BF16)
    acc = x
    for c in range(wup_ref.shape[1] // FFN_CHUNK):
        a = jnp.dot(h, wup_ref[:, c * FFN_CHUNK:(c + 1) * FFN_CHUNK], preferred_element_type=F32)
        a = jnp.square(jnp.maximum(a, 0.0)).astype(BF16)
        acc = acc + jnp.dot(a, wdn_ref[c * FFN_CHUNK:(c + 1) * FFN_CHUNK, :], preferred_element_type=F32)
    y_ref[...] = _rmsnorm(acc, gf_ref[...])


def _ffn(x2d, g_ffn, w_up_bf, w_down_bf, g_final, *, tm):
    m, d = x2d.shape
    row = pl.BlockSpec((tm, d), lambda r: (r, 0))
    return pl.pallas_call(
        _ffn_kernel,
        grid=(m // tm,),
        in_specs=[row, _const_spec((1, d)), _const_spec(w_up_bf.shape), _const_spec(w_down_bf.shape),
                  _const_spec((1, d))],
        out_specs=row,
        out_shape=jax.ShapeDtypeStruct((m, d), F32),
        compiler_params=pltpu.CompilerParams(dimension_semantics=("arbitrary",), vmem_limit_bytes=VMEM_LIMIT),
        name="ffn",
    )(x2d, g_ffn.reshape(1, d), w_up_bf, w_down_bf, g_final.reshape(1, d))


def _ffn_score_kernel(pt_ref, x_ref, g_ref, wup_ref, wdn_ref, gf_ref, qt_ref, ck_hbm, y_ref, sc_ref,
                      kbuf, qb_sc, h_sc, sem_k):
    n_slices = sc_ref.shape[0] * kbuf.shape[0]
    width = wup_ref.shape[1] // n_slices
    x = x_ref[...]
    h_sc[...] = _rmsnorm(x, g_ref[...]).astype(BF16)
    y_ref[...] = x

    def mlp_slice(k):
        a = jnp.dot(h_sc[...], wup_ref[:, k * width:(k + 1) * width], preferred_element_type=F32)
        a = jnp.square(jnp.maximum(a, 0.0)).astype(BF16)
        y_ref[...] += jnp.dot(a, wdn_ref[k * width:(k + 1) * width, :], preferred_element_type=F32)

    _score_cached_keys(pl.program_id(0), pt_ref, qt_ref, ck_hbm, sc_ref, kbuf, qb_sc, sem_k, mlp_slice)
    y_ref[...] = _rmsnorm(y_ref[...], gf_ref[...])


def _ffn_and_sample_scores(x2d, g_ffn, w_up_bf, w_down_bf, g_final, q_t, cache_k, page_table, *, tm):
    m, d = x2d.shape
    s, n_pages = page_table.shape
    page = cache_k.shape[1]
    n_steps = m // tm
    assert s % n_steps == 0 and s <= LANES
    chunk_pages = min(SAMPLE_CHUNK_PAGES, n_pages)
    assert n_pages % chunk_pages == 0
    assert w_up_bf.shape[1] % ((s // n_steps) * (n_pages // chunk_pages) * LANES) == 0
    row = pl.BlockSpec((tm, d), lambda r, pt: (r, 0))
    const = lambda shape: pl.BlockSpec(shape, lambda r, pt: (0,) * len(shape), pipeline_mode=pl.Buffered(1))
    grid_spec = pltpu.PrefetchScalarGridSpec(
        num_scalar_prefetch=1,
        grid=(n_steps,),
        in_specs=[row, const((1, d)), const(w_up_bf.shape), const(w_down_bf.shape), const((1, d)),
                  const((ATTN_W, s)), pl.BlockSpec(memory_space=pl.ANY)],
        out_specs=[row, pl.BlockSpec((s // n_steps, n_pages, N_HEADS, page), lambda r, pt: (r, 0, 0, 0))],
        scratch_shapes=[
            pltpu.VMEM((n_pages // chunk_pages, chunk_pages, ATTN_W, page), F32),
            pltpu.VMEM((ATTN_W, page), F32),
            pltpu.VMEM((tm, d), BF16),
            pltpu.SemaphoreType.DMA((n_pages // chunk_pages,)),
        ],
    )
    return pl.pallas_call(
        _ffn_score_kernel,
        grid_spec=grid_spec,
        out_shape=[jax.ShapeDtypeStruct((m, d), F32), jax.ShapeDtypeStruct((s, n_pages, N_HEADS, page), F32)],
        compiler_params=pltpu.CompilerParams(dimension_semantics=("arbitrary",), vmem_limit_bytes=VMEM_LIMIT),
        name="ffn_scores",
    )(page_table, x2d, g_ffn.reshape(1, d), w_up_bf, w_down_bf, g_final.reshape(1, d), q_t,
      _channel_major_pages(cache_k))


def _rope_tables(pos):
    half = HEAD_DIM // 2
    inv = ROPE_THETA ** (-2.0 * jnp.arange(half, dtype=F32) / HEAD_DIM)
    ang = pos.astype(F32)[:, None] * inv[None, :]
    cos = jnp.tile(jnp.cos(ang), (1, 2 * N_HEADS))
    sin = jnp.tile(jnp.concatenate([-jnp.sin(ang), jnp.sin(ang)], axis=1), (1, N_HEADS))
    return cos, sin


def kernel(x_prompt, x_sample, cache_k, cache_v, page_table, g_attn, w_in, w_pa, w_pb, w_o, g_v, w_s, b_s,
           g_ffn, w_up, w_down, g_final):
    depth = w_in.shape[0]
    assert depth == 1, "the final norm is fused into the last layer's MLP kernel"
    bsz, seq, d = x_prompt.shape
    n_dec, dec_seq, _ = x_sample.shape
    assert dec_seq == 1 and seq % MOBA_BLOCK == 0
    past_len = page_table.shape[1] * cache_k.shape[2]
    n_chunk_rows = seq - ((seq - 1) // SGU_CHUNK) * SGU_CHUNK
    assert n_chunk_rows == SGU_CHUNK

    cos_p, sin_p = _rope_tables(jnp.arange(seq, dtype=jnp.int32))
    cos_s, sin_s = _rope_tables(jnp.full((n_dec,), past_len, dtype=jnp.int32))

    xp = x_prompt.reshape(bsz * seq, d)
    xs = x_sample.reshape(n_dec, d)
    l = 0
    w_in_bf, w_pa_bf, w_pb_bf, w_o_bf = (w[l].astype(BF16) for w in (w_in, w_pa, w_pb, w_o))
    w_up_bf, w_down_bf = w_up[l].astype(BF16), w_down[l].astype(BF16)

    tm_p = PROMPT_ROW_TILE
    assert seq % tm_p == 0
    (q_t, k_p, v_p, kb, vb_t, u, vv, sga, sgb, sguv_p, kmean) = _inproj(
        xp, g_attn[l], w_in_bf, cos_p, sin_p, g_v[l], tm=MOBA_BLOCK, seq_tiles=seq // MOBA_BLOCK, q_dtype=BF16,
        q_scale=PROMPT_Q_SCALE, emit_kmean=True)
    o_a = _prompt_attention(q_t, kb.reshape(bsz, seq, ATTN_W), vb_t,
                            kmean.reshape(bsz, seq // MOBA_BLOCK, ATTN_W)).reshape(bsz * seq, ATTN_W)
    x1 = _merge(xp, o_a, u, vv, sga, sgb, w_s[l], b_s[l], w_pa_bf, w_pb_bf, w_o_bf, tm=tm_p, single_position=False)

    (q_s, k_s, v_s, _, _, u_s, vv_s, sga_s, sgb_s, sguv_s) = _inproj(
        xs, g_attn[l], w_in_bf, cos_s, sin_s, g_v[l], tm=n_dec, seq_tiles=1, q_dtype=F32, q_scale=SCALE,
        emit_kmean=False)
    y_p, scores = _ffn_and_sample_scores(x1, g_ffn[l], w_up_bf, w_down_bf, g_final, q_s[0], cache_k[l], page_table,
                                         tm=tm_p)
    o_as = _sample_attention(scores, q_s[0], k_s[0], v_s[0], cache_v[l], page_table).T.astype(BF16)
    x1_s = _merge(xs, o_as, u_s, vv_s, sga_s, sgb_s, w_s[l], b_s[l], w_pa_bf, w_pb_bf, w_o_bf,
                  tm=n_dec, single_position=True)
    y_s = _ffn(x1_s, g_ffn[l], w_up_bf, w_down_bf, g_final, tm=n_dec)

    def heads_last(t, n, length):
        return t.reshape(1, n, N_HEADS, HEAD_DIM, length).transpose(0, 1, 4, 2, 3)

    return (y_p.reshape(bsz, seq, d),
            y_s.reshape(n_dec, 1, d),
            heads_last(k_p, bsz, seq),
            heads_last(v_p, bsz, seq),
            sguv_p.reshape(1, bsz, SGU_CHUNK, SGU_W),
            heads_last(k_s, 1, n_dec).reshape(1, n_dec, 1, N_HEADS, HEAD_DIM),
            heads_last(v_s, 1, n_dec).reshape(1, n_dec, 1, N_HEADS, HEAD_DIM),
            sguv_s.reshape(1, n_dec, 1, SGU_W))
```

```python
import functools

import jax
import jax.numpy as jnp
from jax import lax
from jax.experimental import pallas as pl
from jax.experimental.pallas import tpu as pltpu

N_HEADS = 8
HEAD_DIM = 64
ATTN_W = N_HEADS * HEAD_DIM
MOBA_BLOCK = 256
MOBA_TOPK = 3
SGU_GROUPS = 8
SGU_W = 512
SGU_GROUP_W = SGU_W // SGU_GROUPS
SGU_CHUNK = 128
ROPE_THETA = 10000.0
EPS = 1e-6
NEG = -1e30
SCALE = HEAD_DIM ** -0.5
LOG2E = 1.4426950408889634
PROMPT_Q_SCALE = SCALE * LOG2E

LANES = 128
F32_SUBLANES = 8
BF16_SUBLANES = 16
HEADS_PER_LANE_BLOCK = LANES // HEAD_DIM
VMEM_LIMIT = 56 * 1024 * 1024
PROMPT_ROW_TILE = 2 * MOBA_BLOCK
SAMPLE_CHUNK_PAGES = 16
SAMPLE_SEQS_PER_STEP = 2
DMA_PRIORITIES = 2

F32 = jnp.float32
BF16 = jnp.bfloat16
NT_DIMS = (((1,), (1,)), ((), ()))


def _rmsnorm(x, g):
    return x * lax.rsqrt(jnp.mean(x * x, axis=-1, keepdims=True) + EPS) * g


def _const_spec(shape):
    return pl.BlockSpec(shape, lambda *_: (0,) * len(shape), pipeline_mode=pl.Buffered(1))


def _rope(t, cos, sin_signed):
    lane = lax.broadcasted_iota(jnp.int32, (t.shape[0], LANES), 1)
    first_half = (lane & (HEAD_DIM - 1)) < HEAD_DIM // 2
    outs = []
    for c in range(t.shape[1] // LANES):
        sl = slice(c * LANES, (c + 1) * LANES)
        tc = t[:, sl]
        partner = jnp.where(first_half, pltpu.roll(tc, LANES - HEAD_DIM // 2, 1),
                            pltpu.roll(tc, HEAD_DIM // 2, 1))
        outs.append(tc * cos[:, sl] + partner * sin_signed[:, sl])
    return jnp.concatenate(outs, axis=1)


def _inproj_kernel(x_ref, g_ref, w_ref, cos_ref, sin_ref, gv_ref,
                   q_ref, kf_ref, vf_ref, kb_ref, vb_ref, u_ref, vv_ref, sga_ref, sgb_ref, sguv_ref,
                   *maybe_kmean_and_scratch, q_scale):
    *maybe_kmean_ref, v_sc = maybe_kmean_and_scratch
    d_model = x_ref.shape[1]
    h = _rmsnorm(x_ref[...], g_ref[...]).astype(BF16)

    def proj(lo, width):
        return jnp.dot(h, w_ref[:, lo:lo + width], preferred_element_type=F32)

    cos = cos_ref[...]
    sin = sin_ref[...]
    q = _rope(proj(0, ATTN_W), cos, sin) * q_scale
    q_ref[0] = q.T.astype(q_ref.dtype)
    k = _rope(proj(ATTN_W, ATTN_W), cos, sin)
    kf_ref[0] = k.T
    kb_ref[...] = k.astype(BF16)
    if maybe_kmean_ref:
        blocks = k.shape[0] // MOBA_BLOCK
        maybe_kmean_ref[0][:, 0, :] = jnp.mean(k.reshape(blocks, MOBA_BLOCK, ATTN_W), axis=1)
    v_sc[...] = proj(2 * ATTN_W, ATTN_W)
    v_t = v_sc[...].T
    vf_ref[0] = v_t
    vb_ref[0] = v_t.astype(BF16)
    base = 3 * ATTN_W
    u_ref[...] = jax.nn.gelu(proj(base, SGU_W)).astype(BF16)
    vv = _rmsnorm(jax.nn.gelu(proj(base + SGU_W, SGU_W)), gv_ref[...])
    vv_ref[...] = vv.astype(BF16)
    sguv_ref[...] = vv[vv.shape[0] - sguv_ref.shape[0]:, :]
    base += 2 * SGU_W
    sga_ref[...] = jax.nn.sigmoid(proj(base, d_model)).astype(BF16)
    sgb_ref[...] = jax.nn.sigmoid(proj(base + d_model, d_model)).astype(BF16)


def _inproj(x2d, g_attn, w_in_bf, cos_tab, sin_tab, g_v, *, tm, seq_tiles, q_dtype, q_scale, emit_kmean):
    m, d = x2d.shape
    n_tiles = m // tm
    n_seq = n_tiles // seq_tiles
    keep = min(SGU_CHUNK, tm)
    row = lambda w: pl.BlockSpec((tm, w), lambda r: (r, 0))
    tab = pl.BlockSpec((tm, ATTN_W), lambda r: (r % seq_tiles, 0))
    chan = pl.BlockSpec((1, ATTN_W, tm), lambda r: (r // seq_tiles, 0, r % seq_tiles))
    chan_shape = lambda dt: jax.ShapeDtypeStruct((n_seq, ATTN_W, seq_tiles * tm), dt)
    out_shape = [
        chan_shape(q_dtype),
        chan_shape(F32),
        chan_shape(F32),
        jax.ShapeDtypeStruct((m, ATTN_W), BF16),
        chan_shape(BF16),
        jax.ShapeDtypeStruct((m, SGU_W), BF16),
        jax.ShapeDtypeStruct((m, SGU_W), BF16),
        jax.ShapeDtypeStruct((m, d), BF16),
        jax.ShapeDtypeStruct((m, d), BF16),
        jax.ShapeDtypeStruct((n_seq * keep, SGU_W), F32),
    ]
    out_specs = [chan, chan, chan, row(ATTN_W), chan] + [row(SGU_W)] * 2 + [row(d)] * 2 + [
        pl.BlockSpec((keep, SGU_W), lambda r: (r // seq_tiles, 0))]
    if emit_kmean:
        assert tm % MOBA_BLOCK == 0
        out_shape.append(jax.ShapeDtypeStruct((m // MOBA_BLOCK, 1, ATTN_W), F32))
        out_specs.append(pl.BlockSpec((tm // MOBA_BLOCK, 1, ATTN_W), lambda r: (r, 0, 0)))
    return pl.pallas_call(
        functools.partial(_inproj_kernel, q_scale=q_scale),
        grid=(n_tiles,),
        in_specs=[row(d), _const_spec((1, d)), _const_spec(w_in_bf.shape), tab, tab, _const_spec((1, SGU_W))],
        out_specs=out_specs,
        out_shape=out_shape,
        scratch_shapes=[pltpu.VMEM((tm, ATTN_W), F32)],
        compiler_params=pltpu.CompilerParams(dimension_semantics=("arbitrary",), vmem_limit_bytes=VMEM_LIMIT),
        name="inproj",
    )(x2d, g_attn.reshape(1, d), w_in_bf, cos_tab, sin_tab, g_v.reshape(1, SGU_W))


def _moba_bias(gate, tile, n_blocks):
    blk = lax.broadcasted_iota(jnp.int32, gate.shape, 0).astype(F32)
    neg = NEG * PROMPT_Q_SCALE
    gate = jnp.where(blk < tile, gate, jnp.where(blk < n_blocks, neg, -jnp.inf))
    if tile <= MOBA_TOPK:
        return jnp.where(gate >= neg, 0.0, jnp.where(blk < tile, NEG, 0.0))
    cand = gate
    for t in range(MOBA_TOPK):
        best = jnp.max(cand, axis=0, keepdims=True)
        idx = jnp.min(jnp.where(cand == best, blk, float(gate.shape[0])), axis=0, keepdims=True)
        if t + 1 < MOBA_TOPK:
            cand = jnp.where(blk == idx, -jnp.inf, cand)
    bias = jnp.where(gate > best, 0.0, jnp.where(gate == best, jnp.where(blk <= idx, 0.0, NEG), NEG))
    return jnp.where(blk >= tile, 0.0, bias)


def _attn_tile(q_ref, k_ref, v_ref, km_ref, o_ref, s_sc, tile):
    tq = MOBA_BLOCK
    g_n = HEADS_PER_LANE_BLOCK
    cols = g_n * tq
    n_blocks = km_ref.shape[1]
    q_t = q_ref[0]
    chan = lax.broadcasted_iota(jnp.int32, (LANES, tq), 0)
    qs = jnp.concatenate([jnp.where(chan // HEAD_DIM == g, q_t, jnp.zeros_like(q_t)) for g in range(g_n)], axis=1)

    bias = None
    if tile > 0:
        pad_rows = -n_blocks % BF16_SUBLANES
        km = jnp.concatenate([km_ref[0], jnp.zeros((pad_rows, LANES), F32)], axis=0)
        km_hi = km.astype(BF16)
        km_lo = (km - km_hi.astype(F32)).astype(BF16)
        gate = (jnp.dot(km_hi, qs, preferred_element_type=F32) + jnp.dot(km_lo, qs, preferred_element_type=F32))
        bias = _moba_bias(gate, tile, n_blocks)

    s_all = jnp.dot(k_ref[0, 0:(tile + 1) * MOBA_BLOCK, :], qs, preferred_element_type=F32)
    kpos = lax.broadcasted_iota(jnp.int32, (MOBA_BLOCK, cols), 0)
    qpos = lax.broadcasted_iota(jnp.int32, (MOBA_BLOCK, cols), 1) & (tq - 1)
    groups = MOBA_BLOCK // F32_SUBLANES
    run_max = None
    for j in range(tile + 1):
        s = s_all[j * MOBA_BLOCK:(j + 1) * MOBA_BLOCK]
        if j == tile:
            s = jnp.where(kpos <= qpos, s, NEG)
        s_sc[j] = s
        part = jnp.max(s.reshape(groups, F32_SUBLANES, cols), axis=0)
        if j < tile:
            part = part + bias[j:j + 1, :]
        run_max = part if run_max is None else jnp.maximum(run_max, part)
    m = jnp.max(run_max, axis=0, keepdims=True)

    acc = None
    run_sum = None
    for j in range(tile + 1):
        shift = m - bias[j:j + 1, :] if j < tile else m
        p = jnp.exp2(s_sc[j] - shift)
        part = jnp.sum(p.reshape(groups, F32_SUBLANES, cols), axis=0)
        run_sum = part if run_sum is None else run_sum + part
        pv = jnp.dot(v_ref[0, :, j * MOBA_BLOCK:(j + 1) * MOBA_BLOCK], p.astype(BF16), preferred_element_type=F32)
        acc = pv if acc is None else acc + pv
    out = acc / jnp.sum(run_sum, axis=0, keepdims=True)
    o_t = jnp.zeros((LANES, tq), F32)
    for g in range(g_n):
        o_t = jnp.where(chan // HEAD_DIM == g, out[:, g * tq:(g + 1) * tq], o_t)
    o_ref[0] = o_t.T.astype(o_ref.dtype)


def _prompt_attn_kernel(q_ref, k_ref, v_ref, km_ref, o_ref, s_sc, *, n_blocks):
    i = pl.program_id(2)
    for tile in range(n_blocks):
        @pl.when(i == tile)
        def _(tile=tile):
            _attn_tile(q_ref, k_ref, v_ref, km_ref, o_ref, s_sc, tile)


def _prompt_attention(q_t, k, v_t, kmean):
    b, l, _ = k.shape
    n_blocks = l // MOBA_BLOCK
    assert n_blocks <= LANES
    cols = HEADS_PER_LANE_BLOCK * MOBA_BLOCK
    return pl.pallas_call(
        functools.partial(_prompt_attn_kernel, n_blocks=n_blocks),
        grid=(b, ATTN_W // LANES, n_blocks),
        in_specs=[
            pl.BlockSpec((1, LANES, MOBA_BLOCK), lambda bi, c, i: (bi, c, i)),
            pl.BlockSpec((1, l, LANES), lambda bi, c, i: (bi, 0, c)),
            pl.BlockSpec((1, LANES, l), lambda bi, c, i: (bi, c, 0)),
            pl.BlockSpec((1, n_blocks, LANES), lambda bi, c, i: (bi, 0, c)),
        ],
        out_specs=pl.BlockSpec((1, MOBA_BLOCK, LANES), lambda bi, c, i: (bi, i, c)),
        out_shape=jax.ShapeDtypeStruct((b, l, ATTN_W), BF16),
        scratch_shapes=[pltpu.VMEM((n_blocks, MOBA_BLOCK, cols), F32)],
        compiler_params=pltpu.CompilerParams(
            dimension_semantics=("arbitrary", "arbitrary", "arbitrary"), vmem_limit_bytes=VMEM_LIMIT),
        name="prompt_attn",
    )(q_t, k, v_t, kmean)


def _column(ref, seq):
    a = ref[...]
    seq_id = lax.broadcasted_iota(jnp.int32, a.shape, 1)
    return jnp.sum(jnp.where(seq_id == seq, a, 0.0), axis=1, keepdims=True)


def _page_scores(kt, q_bcast):
    return jnp.sum((kt * q_bcast).reshape(N_HEADS, HEAD_DIM, kt.shape[1]), axis=1)


def _score_cached_keys(step, pt_ref, qt_ref, ck_hbm, sc_ref, kbuf, qb_sc, sem_k, host_work):
    seqs_per_step = sc_ref.shape[0]
    n_seq = pl.num_programs(0) * seqs_per_step
    cps, chunk_pages = kbuf.shape[0], kbuf.shape[1]

    def k_copy(seq, c, p):
        return pltpu.make_async_copy(ck_hbm.at[pt_ref[seq, c * chunk_pages + p]], kbuf.at[c, p], sem_k.at[c])

    def start_chunk(seq, c):
        for p in range(chunk_pages):
            k_copy(seq, c, p).start(priority=p % DMA_PRIORITIES)

    @pl.when(step == 0)
    def _():
        for c in range(cps):
            start_chunk(0, c)

    for i in range(seqs_per_step):
        seq = step * seqs_per_step + i
        qb_sc[...] = jnp.broadcast_to(_column(qt_ref, seq), qb_sc.shape)
        for c in range(cps):
            for p in range(chunk_pages):
                k_copy(seq, c, p).wait()

            for p in range(chunk_pages):
                sc_ref[i, c * chunk_pages + p] = _page_scores(kbuf[c, p], qb_sc[...])
            host_work(i * cps + c)

            @pl.when(seq + 1 < n_seq)
            def _(seq=seq, c=c):
                start_chunk(seq + 1, c)


def _sample_attn_kernel(pt_ref, sc_ref, q_ref, kn_ref, vn_ref, cv_hbm, o_ref,
                        vbuf, s_sc, sem_v, *, n_pages, page):
    step = pl.program_id(0)
    group = sc_ref.shape[0]

    @pl.when(step == 0)
    def _():
        o_ref[...] = jnp.zeros_like(o_ref)

    selections = [_sample_select(step * group + g, g, pt_ref, sc_ref, q_ref, kn_ref, cv_hbm, vbuf, sem_v,
                                 n_pages=n_pages, page=page) for g in range(group)]
    for g, sel in enumerate(selections):
        _sample_finish(step * group + g, g, sel, sc_ref, vn_ref, o_ref, vbuf, s_sc, n_pages=n_pages, page=page)


def _sample_select(b, g, pt_ref, sc_ref, q_ref, kn_ref, cv_hbm, vbuf, sem_v, *, n_pages, page):
    ppb = MOBA_BLOCK // page
    n_past = n_pages // ppb
    n_vbuf = vbuf.shape[0] // sc_ref.shape[0]
    q_col, kn_col = _column(q_ref, b), _column(kn_ref, b)

    pos_w = lax.broadcasted_iota(jnp.int32, (ATTN_W, page), 1)
    s_self = _page_scores(jnp.where(pos_w == 0, kn_col, 0.0), q_col)

    hl = lax.broadcasted_iota(jnp.int32, (N_HEADS, LANES), 1)
    cand = jnp.where(hl == n_past, NEG * SCALE, -jnp.inf).astype(F32)
    for j in range(n_past):
        blk = sc_ref[g, j * ppb]
        for r in range(1, ppb):
            blk = blk + sc_ref[g, j * ppb + r]
        cand = jnp.where(hl == j, jnp.sum(blk, axis=1, keepdims=True) * (1.0 / MOBA_BLOCK), cand)
    rank = jnp.zeros((N_HEADS, LANES), F32)
    for j in range(n_past + 1):
        gj = cand[:, j:j + 1]
        rank = rank + jnp.where(gj > cand, 1.0, jnp.where(gj == cand, jnp.where(hl > j, 1.0, 0.0), 0.0))
    hl_f = hl.astype(F32)
    picks = [jnp.sum(jnp.where(rank == t, hl_f, 0.0), axis=1, keepdims=True).astype(jnp.int32)
             for t in range(MOBA_TOPK)]

    blocks = [[picks[t][h, 0] for t in range(MOBA_TOPK)] for h in range(N_HEADS)]
    clamped = [[jnp.minimum(blk, n_past - 1) for blk in row] for row in blocks]

    def v_copy(h, t, r):
        phys = pt_ref[b, clamped[h][t] * ppb + r]
        return pltpu.make_async_copy(cv_hbm.at[phys, pl.ds(h * HEAD_DIM, HEAD_DIM), :],
                                     vbuf.at[g * n_vbuf + (h * MOBA_TOPK + t) * ppb + r], sem_v.at[g])

    copies = [v_copy(h, t, r) for h in range(N_HEADS) for t in range(MOBA_TOPK) for r in range(ppb)]
    for n, copy in enumerate(copies):
        copy.start(priority=n % DMA_PRIORITIES)
    return dict(s_self=s_self, picks=picks, blocks=blocks, clamped=clamped, copies=copies)


def _sample_finish(b, g, sel, sc_ref, vn_ref, o_ref, vbuf, s_sc, *, n_pages, page):
    ppb = MOBA_BLOCK // page
    n_past = n_pages // ppb
    n_vbuf = vbuf.shape[0] // sc_ref.shape[0]
    s_self, picks, blocks, clamped = sel["s_self"], sel["picks"], sel["blocks"], sel["clamped"]
    hl = lax.broadcasted_iota(jnp.int32, (N_HEADS, LANES), 1)
    vn_col = _column(vn_ref, b)

    s_all = sc_ref[g]
    blk_id = lax.broadcasted_iota(jnp.int32, s_all.shape, 0) // ppb
    chosen = blk_id == picks[0][None]
    for t in range(1, MOBA_TOPK):
        chosen = chosen | (blk_id == picks[t][None])
    s_all = jnp.where(chosen, s_all, NEG)
    s_self = jnp.where(hl == 0, s_self, NEG)
    m = jnp.max(jnp.maximum(jnp.max(s_all, axis=0), s_self), axis=1, keepdims=True)
    p_all = jnp.exp(s_all - m[None])
    p_self = jnp.exp(s_self - m)
    denom = jnp.sum(jnp.sum(p_all, axis=0) + p_self, axis=1, keepdims=True)
    s_sc[g] = p_all / denom[None]
    p_self = p_self / denom

    for copy in sel["copies"]:
        copy.wait()

    pos_h = lax.broadcasted_iota(jnp.int32, (HEAD_DIM, page), 1)
    cols = []
    for h in range(N_HEADS):
        rows = slice(h * HEAD_DIM, (h + 1) * HEAD_DIM)
        acc = p_self[h:h + 1, :] * jnp.where(pos_h == 0, vn_col[rows], 0.0)
        for t in range(MOBA_TOPK):
            for r in range(ppb):
                p_row = s_sc[g, clamped[h][t] * ppb + r, pl.ds(h, 1), :]
                p_row = jnp.where(blocks[h][t] < n_past, p_row, 0.0)
                acc = acc + p_row * vbuf[g * n_vbuf + (h * MOBA_TOPK + t) * ppb + r]
        cols.append(jnp.sum(acc, axis=1, keepdims=True))
    o_col = jnp.concatenate(cols, axis=0)
    seq_id = lax.broadcasted_iota(jnp.int32, o_ref.shape, 1)
    o_ref[...] = jnp.where(seq_id == b, o_col, o_ref[...])


def _channel_major_pages(cache):
    n_phys, page = cache.shape[0], cache.shape[1]
    return cache.transpose(0, 2, 3, 1).reshape(n_phys, ATTN_W, page)


def _sample_attention(scores, q_t, k_new_t, v_new_t, cache_v, page_table):
    s, n_pages = page_table.shape
    page = cache_v.shape[1]
    assert page == LANES and MOBA_BLOCK % page == 0 and (n_pages * page) % MOBA_BLOCK == 0
    n_past = n_pages * page // MOBA_BLOCK
    assert 1 <= n_past < LANES and s <= LANES
    n_vbuf = N_HEADS * MOBA_TOPK * (MOBA_BLOCK // page)
    group = SAMPLE_SEQS_PER_STEP if s % SAMPLE_SEQS_PER_STEP == 0 else 1
    tok = pl.BlockSpec((ATTN_W, s), lambda b, pt: (0, 0))
    grid_spec = pltpu.PrefetchScalarGridSpec(
        num_scalar_prefetch=1,
        grid=(s // group,),
        in_specs=[pl.BlockSpec((group, n_pages, N_HEADS, page), lambda b, pt: (b, 0, 0, 0)), tok, tok, tok,
                  pl.BlockSpec(memory_space=pl.ANY)],
        out_specs=tok,
        scratch_shapes=[
            pltpu.VMEM((group * n_vbuf, HEAD_DIM, page), F32),
            pltpu.VMEM((group, n_pages, N_HEADS, page), F32),
            pltpu.SemaphoreType.DMA((group,)),
        ],
    )
    return pl.pallas_call(
        functools.partial(_sample_attn_kernel, n_pages=n_pages, page=page),
        grid_spec=grid_spec,
        out_shape=jax.ShapeDtypeStruct((ATTN_W, s), F32),
        compiler_params=pltpu.CompilerParams(dimension_semantics=("arbitrary",), vmem_limit_bytes=VMEM_LIMIT),
        name="sample_attn",
    )(page_table, scores, q_t, k_new_t, v_new_t, _channel_major_pages(cache_v))


def _merge_kernel(x_ref, oa_ref, u_ref, vv_ref, sga_ref, sgb_ref, ws_ref, bs_ref, wpa_ref, wpb_ref, wo_ref,
                  x1_ref, *, single_position):
    tm = x_ref.shape[0]
    if single_position:
        s = vv_ref[...].astype(F32) * ws_ref[...] + bs_ref[...]
    else:
        t_idx = lax.broadcasted_iota(jnp.int32, (SGU_CHUNK, SGU_CHUNK), 0)
        s_idx = lax.broadcasted_iota(jnp.int32, (SGU_CHUNK, SGU_CHUNK), 1)
        ws = [jnp.where(s_idx <= t_idx, ws_ref[g], 0.0).astype(BF16) for g in range(SGU_GROUPS)]
        lane = lax.broadcasted_iota(jnp.int32, (SGU_CHUNK, LANES), 1)
        groups_per_block = LANES // SGU_GROUP_W
        chunks = []
        for c in range(tm // SGU_CHUNK):
            parts = []
            for blk in range(SGU_W // LANES):
                vp = vv_ref[c * SGU_CHUNK:(c + 1) * SGU_CHUNK, blk * LANES:(blk + 1) * LANES]
                sp = jnp.zeros((SGU_CHUNK, LANES), F32)
                for g in range(groups_per_block):
                    sg = jnp.dot(ws[blk * groups_per_block + g], vp, preferred_element_type=F32)
                    sp = jnp.where(lane // SGU_GROUP_W == g, sg, sp)
                parts.append(sp)
            chunks.append(jnp.concatenate(parts, axis=1) + bs_ref[...])
        s = jnp.concatenate(chunks, axis=0)
    ob = (u_ref[...].astype(F32) * s).astype(BF16)
    ba = jnp.dot(oa_ref[...], wpa_ref[...], preferred_element_type=F32)
    bb = jnp.dot(ob, wpb_ref[...], preferred_element_type=F32)
    merged = (sga_ref[...].astype(F32) * ba + sgb_ref[...].astype(F32) * bb).astype(BF16)
    x1_ref[...] = x_ref[...] + jnp.dot(merged, wo_ref[...], preferred_element_type=F32)


def _merge(x2d, o_a, u, vv, sga, sgb, w_s, b_s, w_pa_bf, w_pb_bf, w_o_bf, *, tm, single_position):
    m, d = x2d.shape
    if single_position:
        ws_in = jnp.repeat(w_s[:, 0, 0], SGU_GROUP_W).reshape(1, SGU_W)
        bs_in = jnp.repeat(b_s[:, 0], SGU_GROUP_W).reshape(1, SGU_W)
    else:
        ws_in = w_s
        bs_in = jnp.repeat(b_s.T, SGU_GROUP_W, axis=1)
    row = lambda w: pl.BlockSpec((tm, w), lambda r: (r, 0))
    return pl.pallas_call(
        functools.partial(_merge_kernel, single_position=single_position),
        grid=(m // tm,),
        in_specs=[row(d), row(ATTN_W), row(SGU_W), row(SGU_W), row(d), row(d),
                  _const_spec(ws_in.shape), _const_spec(bs_in.shape),
                  _const_spec(w_pa_bf.shape), _const_spec(w_pb_bf.shape), _const_spec(w_o_bf.shape)],
        out_specs=row(d),
        out_shape=jax.ShapeDtypeStruct((m, d), F32),
        compiler_params=pltpu.CompilerParams(dimension_semantics=("arbitrary",), vmem_limit_bytes=VMEM_LIMIT),
        name="merge",
    )(x2d, o_a, u, vv, sga, sgb, ws_in, bs_in, w_pa_bf, w_pb_bf, w_o_bf)


FFN_CHUNK = 1024


def _ffn_kernel(x_ref, g_ref, wup_ref, wdn_ref, gf_ref, y_ref):
    x = x_ref[...]
    h = _rmsnorm(x, g_ref[...]).astype(BF16)
    acc = x
    for c in range(wup_ref.shape[1] // FFN_CHUNK):
        a = jnp.dot(h, wup_ref[:, c * FFN_CHUNK:(c + 1) * FFN_CHUNK], preferred_element_type=F32)
        a = jnp.square(jnp.maximum(a, 0.0)).astype(BF16)
        acc = acc + jnp.dot(a, wdn_ref[c * FFN_CHUNK:(c + 1) * FFN_CHUNK, :], preferred_element_type=F32)
    y_ref[...] = _rmsnorm(acc, gf_ref[...])


def _ffn(x2d, g_ffn, w_up_bf, w_down_bf, g_final, *, tm):
    m, d = x2d.shape
    row = pl.BlockSpec((tm, d), lambda r: (r, 0))
    return pl.pallas_call(
        _ffn_kernel,
        grid=(m // tm,),
        in_specs=[row, _const_spec((1, d)), _const_spec(w_up_bf.shape), _const_spec(w_down_bf.shape),
                  _const_spec((1, d))],
        out_specs=row,
        out_shape=jax.ShapeDtypeStruct((m, d), F32),
        compiler_params=pltpu.CompilerParams(dimension_semantics=("arbitrary",), vmem_limit_bytes=VMEM_LIMIT),
        name="ffn",
    )(x2d, g_ffn.reshape(1, d), w_up_bf, w_down_bf, g_final.reshape(1, d))


def _ffn_score_kernel(pt_ref, x_ref, g_ref, wup_ref, wdn_ref, gf_ref, qt_ref, ck_hbm, y_ref, sc_ref,
                      kbuf, qb_sc, h_sc, sem_k):
    n_slices = sc_ref.shape[0] * kbuf.shape[0]
    width = wup_ref.shape[1] // n_slices
    x = x_ref[...]
    h_sc[...] = _rmsnorm(x, g_ref[...]).astype(BF16)
    y_ref[...] = x

    def mlp_slice(k):
        a = jnp.dot(h_sc[...], wup_ref[:, k * width:(k + 1) * width], preferred_element_type=F32)
        a = jnp.square(jnp.maximum(a, 0.0)).astype(BF16)
        y_ref[...] += jnp.dot(a, wdn_ref[k * width:(k + 1) * width, :], preferred_element_type=F32)

    _score_cached_keys(pl.program_id(0), pt_ref, qt_ref, ck_hbm, sc_ref, kbuf, qb_sc, sem_k, mlp_slice)
    y_ref[...] = _rmsnorm(y_ref[...], gf_ref[...])


def _ffn_and_sample_scores(x2d, g_ffn, w_up_bf, w_down_bf, g_final, q_t, cache_k, page_table, *, tm):
    m, d = x2d.shape
    s, n_pages = page_table.shape
    page = cache_k.shape[1]
    n_steps = m // tm
    assert s % n_steps == 0 and s <= LANES
    chunk_pages = min(SAMPLE_CHUNK_PAGES, n_pages)
    assert n_pages % chunk_pages == 0
    assert w_up_bf.shape[1] % ((s // n_steps) * (n_pages // chunk_pages) * LANES) == 0
    row = pl.BlockSpec((tm, d), lambda r, pt: (r, 0))
    const = lambda shape: pl.BlockSpec(shape, lambda r, pt: (0,) * len(shape), pipeline_mode=pl.Buffered(1))
    grid_spec = pltpu.PrefetchScalarGridSpec(
        num_scalar_prefetch=1,
        grid=(n_steps,),
        in_specs=[row, const((1, d)), const(w_up_bf.shape), const(w_down_bf.shape), const((1, d)),
                  const((ATTN_W, s)), pl.BlockSpec(memory_space=pl.ANY)],
        out_specs=[row, pl.BlockSpec((s // n_steps, n_pages, N_HEADS, page), lambda r, pt: (r, 0, 0, 0))],
        scratch_shapes=[
            pltpu.VMEM((n_pages // chunk_pages, chunk_pages, ATTN_W, page), F32),
            pltpu.VMEM((ATTN_W, page), F32),
            pltpu.VMEM((tm, d), BF16),
            pltpu.SemaphoreType.DMA((n_pages // chunk_pages,)),
        ],
    )
    return pl.pallas_call(
        _ffn_score_kernel,
        grid_spec=grid_spec,
        out_shape=[jax.ShapeDtypeStruct((m, d), F32), jax.ShapeDtypeStruct((s, n_pages, N_HEADS, page), F32)],
        compiler_params=pltpu.CompilerParams(dimension_semantics=("arbitrary",), vmem_limit_bytes=VMEM_LIMIT),
        name="ffn_scores",
    )(page_table, x2d, g_ffn.reshape(1, d), w_up_bf, w_down_bf, g_final.reshape(1, d), q_t,
      _channel_major_pages(cache_k))


def _rope_tables(pos):
    half = HEAD_DIM // 2
    inv = ROPE_THETA ** (-2.0 * jnp.arange(half, dtype=F32) / HEAD_DIM)
    ang = pos.astype(F32)[:, None] * inv[None, :]
    cos = jnp.tile(jnp.cos(ang), (1, 2 * N_HEADS))
    sin = jnp.tile(jnp.concatenate([-jnp.sin(ang), jnp.sin(ang)], axis=1), (1, N_HEADS))
    return cos, sin


def kernel(x_prompt, x_sample, cache_k, cache_v, page_table, g_attn, w_in, w_pa, w_pb, w_o, g_v, w_s, b_s,
           g_ffn, w_up, w_down, g_final):
    depth = w_in.shape[0]
    assert depth == 1, "the final norm is fused into the last layer's MLP kernel"
    bsz, seq, d = x_prompt.shape
    n_dec, dec_seq, _ = x_sample.shape
    assert dec_seq == 1 and seq % MOBA_BLOCK == 0
    past_len = page_table.shape[1] * cache_k.shape[2]
    n_chunk_rows = seq - ((seq - 1) // SGU_CHUNK) * SGU_CHUNK
    assert n_chunk_rows == SGU_CHUNK

    cos_p, sin_p = _rope_tables(jnp.arange(seq, dtype=jnp.int32))
    cos_s, sin_s = _rope_tables(jnp.full((n_dec,), past_len, dtype=jnp.int32))

    xp = x_prompt.reshape(bsz * seq, d)
    xs = x_sample.reshape(n_dec, d)
    l = 0
    w_in_bf, w_pa_bf, w_pb_bf, w_o_bf = (w[l].astype(BF16) for w in (w_in, w_pa, w_pb, w_o))
    w_up_bf, w_down_bf = w_up[l].astype(BF16), w_down[l].astype(BF16)

    tm_p = PROMPT_ROW_TILE
    assert seq % tm_p == 0
    (q_t, k_p, v_p, kb, vb_t, u, vv, sga, sgb, sguv_p, kmean) = _inproj(
        xp, g_attn[l], w_in_bf, cos_p, sin_p, g_v[l], tm=MOBA_BLOCK, seq_tiles=seq // MOBA_BLOCK, q_dtype=BF16,
        q_scale=PROMPT_Q_SCALE, emit_kmean=True)
    o_a = _prompt_attention(q_t, kb.reshape(bsz, seq, ATTN_W), vb_t,
                            kmean.reshape(bsz, seq // MOBA_BLOCK, ATTN_W)).reshape(bsz * seq, ATTN_W)
    x1 = _merge(xp, o_a, u, vv, sga, sgb, w_s[l], b_s[l], w_pa_bf, w_pb_bf, w_o_bf, tm=tm_p, single_position=False)

    (q_s, k_s, v_s, _, _, u_s, vv_s, sga_s, sgb_s, sguv_s) = _inproj(
        xs, g_attn[l], w_in_bf, cos_s, sin_s, g_v[l], tm=n_dec, seq_tiles=1, q_dtype=F32, q_scale=SCALE,
        emit_kmean=False)
    y_p, scores = _ffn_and_sample_scores(x1, g_ffn[l], w_up_bf, w_down_bf, g_final, q_s[0], cache_k[l], page_table,
                                         tm=tm_p)
    o_as = _sample_attention(scores, q_s[0], k_s[0], v_s[0], cache_v[l], page_table).T.astype(BF16)
    x1_s = _merge(xs, o_as, u_s, vv_s, sga_s, sgb_s, w_s[l], b_s[l], w_pa_bf, w_pb_bf, w_o_bf,
                  tm=n_dec, single_position=True)
    y_s = _ffn(x1_s, g_ffn[l], w_up_bf, w_down_bf, g_final, tm=n_dec)

    def heads_last(t, n, length):
        return t.reshape(1, n, N_HEADS, HEAD_DIM, length).transpose(0, 1, 4, 2, 3)

    return (y_p.reshape(bsz, seq, d),
            y_s.reshape(n_dec, 1, d),
            heads_last(k_p, bsz, seq),
            heads_last(v_p, bsz, seq),
            sguv_p.reshape(1, bsz, SGU_CHUNK, SGU_W),
            heads_last(k_s, 1, n_dec).reshape(1, n_dec, 1, N_HEADS, HEAD_DIM),
            heads_last(v_s, 1, n_dec).reshape(1, n_dec, 1, N_HEADS, HEAD_DIM),
            sguv_s.reshape(1, n_dec, 1, SGU_W))
```

```python
import functools

import jax
import jax.numpy as jnp
from jax import lax
from jax.experimental import pallas as pl
from jax.experimental.pallas import tpu as pltpu

N_HEADS = 8
HEAD_DIM = 64
ATTN_W = N_HEADS * HEAD_DIM
MOBA_BLOCK = 256
MOBA_TOPK = 3
SGU_GROUPS = 8
SGU_W = 512
SGU_GROUP_W = SGU_W // SGU_GROUPS
SGU_CHUNK = 128
ROPE_THETA = 10000.0
EPS = 1e-6
NEG = -1e30
SCALE = HEAD_DIM ** -0.5
LOG2E = 1.4426950408889634
PROMPT_Q_SCALE = SCALE * LOG2E

LANES = 128
F32_SUBLANES = 8
BF16_SUBLANES = 16
HEADS_PER_LANE_BLOCK = LANES // HEAD_DIM
VMEM_LIMIT = 56 * 1024 * 1024
PROMPT_ROW_TILE = 2 * MOBA_BLOCK
SAMPLE_CHUNK_PAGES = 16
SAMPLE_SEQS_PER_STEP = 4
DMA_PRIORITIES = 2

F32 = jnp.float32
BF16 = jnp.bfloat16
NT_DIMS = (((1,), (1,)), ((), ()))


def _rmsnorm(x, g):
    return x * lax.rsqrt(jnp.mean(x * x, axis=-1, keepdims=True) + EPS) * g


def _const_spec(shape):
    return pl.BlockSpec(shape, lambda *_: (0,) * len(shape), pipeline_mode=pl.Buffered(1))


def _rope(t, cos, sin_signed):
    lane = lax.broadcasted_iota(jnp.int32, (t.shape[0], LANES), 1)
    first_half = (lane & (HEAD_DIM - 1)) < HEAD_DIM // 2
    outs = []
    for c in range(t.shape[1] // LANES):
        sl = slice(c * LANES, (c + 1) * LANES)
        tc = t[:, sl]
        partner = jnp.where(first_half, pltpu.roll(tc, LANES - HEAD_DIM // 2, 1),
                            pltpu.roll(tc, HEAD_DIM // 2, 1))
        outs.append(tc * cos[:, sl] + partner * sin_signed[:, sl])
    return jnp.concatenate(outs, axis=1)


def _inproj_kernel(x_ref, g_ref, w_ref, cos_ref, sin_ref, gv_ref,
                   q_ref, kf_ref, vf_ref, kb_ref, vb_ref, u_ref, vv_ref, sga_ref, sgb_ref, sguv_ref,
                   *maybe_kmean_and_scratch, q_scale):
    *maybe_kmean_ref, v_sc = maybe_kmean_and_scratch
    d_model = x_ref.shape[1]
    h = _rmsnorm(x_ref[...], g_ref[...]).astype(BF16)

    def proj(lo, width):
        return jnp.dot(h, w_ref[:, lo:lo + width], preferred_element_type=F32)

    cos = cos_ref[...]
    sin = sin_ref[...]
    q = _rope(proj(0, ATTN_W), cos, sin) * q_scale
    q_ref[0] = q.T.astype(q_ref.dtype)
    k = _rope(proj(ATTN_W, ATTN_W), cos, sin)
    kf_ref[0] = k.T
    kb_ref[...] = k.astype(BF16)
    if maybe_kmean_ref:
        blocks = k.shape[0] // MOBA_BLOCK
        maybe_kmean_ref[0][:, 0, :] = jnp.mean(k.reshape(blocks, MOBA_BLOCK, ATTN_W), axis=1)
    v_sc[...] = proj(2 * ATTN_W, ATTN_W)
    v_t = v_sc[...].T
    vf_ref[0] = v_t
    vb_ref[0] = v_t.astype(BF16)
    base = 3 * ATTN_W
    u_ref[...] = jax.nn.gelu(proj(base, SGU_W)).astype(BF16)
    vv = _rmsnorm(jax.nn.gelu(proj(base + SGU_W, SGU_W)), gv_ref[...])
    vv_ref[...] = vv.astype(BF16)
    sguv_ref[...] = vv[vv.shape[0] - sguv_ref.shape[0]:, :]
    base += 2 * SGU_W
    sga_ref[...] = jax.nn.sigmoid(proj(base, d_model)).astype(BF16)
    sgb_ref[...] = jax.nn.sigmoid(proj(base + d_model, d_model)).astype(BF16)


def _inproj(x2d, g_attn, w_in_bf, cos_tab, sin_tab, g_v, *, tm, seq_tiles, q_dtype, q_scale, emit_kmean):
    m, d = x2d.shape
    n_tiles = m // tm
    n_seq = n_tiles // seq_tiles
    keep = min(SGU_CHUNK, tm)
    row = lambda w: pl.BlockSpec((tm, w), lambda r: (r, 0))
    tab = pl.BlockSpec((tm, ATTN_W), lambda r: (r % seq_tiles, 0))
    chan = pl.BlockSpec((1, ATTN_W, tm), lambda r: (r // seq_tiles, 0, r % seq_tiles))
    chan_shape = lambda dt: jax.ShapeDtypeStruct((n_seq, ATTN_W, seq_tiles * tm), dt)
    out_shape = [
        chan_shape(q_dtype),
        chan_shape(F32),
        chan_shape(F32),
        jax.ShapeDtypeStruct((m, ATTN_W), BF16),
        chan_shape(BF16),
        jax.ShapeDtypeStruct((m, SGU_W), BF16),
        jax.ShapeDtypeStruct((m, SGU_W), BF16),
        jax.ShapeDtypeStruct((m, d), BF16),
        jax.ShapeDtypeStruct((m, d), BF16),
        jax.ShapeDtypeStruct((n_seq * keep, SGU_W), F32),
    ]
    out_specs = [chan, chan, chan, row(ATTN_W), chan] + [row(SGU_W)] * 2 + [row(d)] * 2 + [
        pl.BlockSpec((keep, SGU_W), lambda r: (r // seq_tiles, 0))]
    if emit_kmean:
        assert tm % MOBA_BLOCK == 0
        out_shape.append(jax.ShapeDtypeStruct((m // MOBA_BLOCK, 1, ATTN_W), F32))
        out_specs.append(pl.BlockSpec((tm // MOBA_BLOCK, 1, ATTN_W), lambda r: (r, 0, 0)))
    return pl.pallas_call(
        functools.partial(_inproj_kernel, q_scale=q_scale),
        grid=(n_tiles,),
        in_specs=[row(d), _const_spec((1, d)), _const_spec(w_in_bf.shape), tab, tab, _const_spec((1, SGU_W))],
        out_specs=out_specs,
        out_shape=out_shape,
        scratch_shapes=[pltpu.VMEM((tm, ATTN_W), F32)],
        compiler_params=pltpu.CompilerParams(dimension_semantics=("arbitrary",), vmem_limit_bytes=VMEM_LIMIT),
        name="inproj",
    )(x2d, g_attn.reshape(1, d), w_in_bf, cos_tab, sin_tab, g_v.reshape(1, SGU_W))


def _moba_bias(gate, tile, n_blocks):
    blk = lax.broadcasted_iota(jnp.int32, gate.shape, 0).astype(F32)
    neg = NEG * PROMPT_Q_SCALE
    gate = jnp.where(blk < tile, gate, jnp.where(blk < n_blocks, neg, -jnp.inf))
    if tile <= MOBA_TOPK:
        return jnp.where(gate >= neg, 0.0, jnp.where(blk < tile, NEG, 0.0))
    cand = gate
    for t in range(MOBA_TOPK):
        best = jnp.max(cand, axis=0, keepdims=True)
        idx = jnp.min(jnp.where(cand == best, blk, float(gate.shape[0])), axis=0, keepdims=True)
        if t + 1 < MOBA_TOPK:
            cand = jnp.where(blk == idx, -jnp.inf, cand)
    bias = jnp.where(gate > best, 0.0, jnp.where(gate == best, jnp.where(blk <= idx, 0.0, NEG), NEG))
    return jnp.where(blk >= tile, 0.0, bias)


def _attn_tile(q_ref, k_ref, v_ref, km_ref, o_ref, s_sc, tile):
    tq = MOBA_BLOCK
    g_n = HEADS_PER_LANE_BLOCK
    cols = g_n * tq
    n_blocks = km_ref.shape[1]
    q_t = q_ref[0]
    chan = lax.broadcasted_iota(jnp.int32, (LANES, tq), 0)
    qs = jnp.concatenate([jnp.where(chan // HEAD_DIM == g, q_t, jnp.zeros_like(q_t)) for g in range(g_n)], axis=1)

    bias = None
    if tile > 0:
        pad_rows = -n_blocks % BF16_SUBLANES
        km = jnp.concatenate([km_ref[0], jnp.zeros((pad_rows, LANES), F32)], axis=0)
        km_hi = km.astype(BF16)
        km_lo = (km - km_hi.astype(F32)).astype(BF16)
        gate = (jnp.dot(km_hi, qs, preferred_element_type=F32) + jnp.dot(km_lo, qs, preferred_element_type=F32))
        bias = _moba_bias(gate, tile, n_blocks)

    s_all = jnp.dot(k_ref[0, 0:(tile + 1) * MOBA_BLOCK, :], qs, preferred_element_type=F32)
    kpos = lax.broadcasted_iota(jnp.int32, (MOBA_BLOCK, cols), 0)
    qpos = lax.broadcasted_iota(jnp.int32, (MOBA_BLOCK, cols), 1) & (tq - 1)
    groups = MOBA_BLOCK // F32_SUBLANES
    run_max = None
    for j in range(tile + 1):
        s = s_all[j * MOBA_BLOCK:(j + 1) * MOBA_BLOCK]
        if j == tile:
            s = jnp.where(kpos <= qpos, s, NEG)
        s_sc[j] = s
        part = jnp.max(s.reshape(groups, F32_SUBLANES, cols), axis=0)
        if j < tile:
            part = part + bias[j:j + 1, :]
        run_max = part if run_max is None else jnp.maximum(run_max, part)
    m = jnp.max(run_max, axis=0, keepdims=True)

    acc = None
    run_sum = None
    for j in range(tile + 1):
        shift = m - bias[j:j + 1, :] if j < tile else m
        p = jnp.exp2(s_sc[j] - shift)
        part = jnp.sum(p.reshape(groups, F32_SUBLANES, cols), axis=0)
        run_sum = part if run_sum is None else run_sum + part
        pv = jnp.dot(v_ref[0, :, j * MOBA_BLOCK:(j + 1) * MOBA_BLOCK], p.astype(BF16), preferred_element_type=F32)
        acc = pv if acc is None else acc + pv
    out = acc / jnp.sum(run_sum, axis=0, keepdims=True)
    o_t = jnp.zeros((LANES, tq), F32)
    for g in range(g_n):
        o_t = jnp.where(chan // HEAD_DIM == g, out[:, g * tq:(g + 1) * tq], o_t)
    o_ref[0] = o_t.T.astype(o_ref.dtype)


def _prompt_attn_kernel(q_ref, k_ref, v_ref, km_ref, o_ref, s_sc, *, n_blocks):
    i = pl.program_id(2)
    for tile in range(n_blocks):
        @pl.when(i == tile)
        def _(tile=tile):
            _attn_tile(q_ref, k_ref, v_ref, km_ref, o_ref, s_sc, tile)


def _prompt_attention(q_t, k, v_t, kmean):
    b, l, _ = k.shape
    n_blocks = l // MOBA_BLOCK
    assert n_blocks <= LANES
    cols = HEADS_PER_LANE_BLOCK * MOBA_BLOCK
    return pl.pallas_call(
        functools.partial(_prompt_attn_kernel, n_blocks=n_blocks),
        grid=(b, ATTN_W // LANES, n_blocks),
        in_specs=[
            pl.BlockSpec((1, LANES, MOBA_BLOCK), lambda bi, c, i: (bi, c, i)),
            pl.BlockSpec((1, l, LANES), lambda bi, c, i: (bi, 0, c)),
            pl.BlockSpec((1, LANES, l), lambda bi, c, i: (bi, c, 0)),
            pl.BlockSpec((1, n_blocks, LANES), lambda bi, c, i: (bi, 0, c)),
        ],
        out_specs=pl.BlockSpec((1, MOBA_BLOCK, LANES), lambda bi, c, i: (bi, i, c)),
        out_shape=jax.ShapeDtypeStruct((b, l, ATTN_W), BF16),
        scratch_shapes=[pltpu.VMEM((n_blocks, MOBA_BLOCK, cols), F32)],
        compiler_params=pltpu.CompilerParams(
            dimension_semantics=("arbitrary", "arbitrary", "arbitrary"), vmem_limit_bytes=VMEM_LIMIT),
        name="prompt_attn",
    )(q_t, k, v_t, kmean)


def _column(ref, seq):
    a = ref[...]
    seq_id = lax.broadcasted_iota(jnp.int32, a.shape, 1)
    return jnp.sum(jnp.where(seq_id == seq, a, 0.0), axis=1, keepdims=True)


def _page_scores(kt, q_bcast):
    return jnp.sum((kt * q_bcast).reshape(N_HEADS, HEAD_DIM, kt.shape[1]), axis=1)


def _score_cached_keys(step, pt_ref, qt_ref, ck_hbm, sc_ref, kbuf, qb_sc, sem_k, host_work):
    seqs_per_step = sc_ref.shape[0]
    n_seq = pl.num_programs(0) * seqs_per_step
    cps, chunk_pages = kbuf.shape[0], kbuf.shape[1]

    def k_copy(seq, c, p):
        return pltpu.make_async_copy(ck_hbm.at[pt_ref[seq, c * chunk_pages + p]], kbuf.at[c, p], sem_k.at[c])

    def start_chunk(seq, c):
        for p in range(chunk_pages):
            k_copy(seq, c, p).start(priority=p % DMA_PRIORITIES)

    @pl.when(step == 0)
    def _():
        for c in range(cps):
            start_chunk(0, c)

    for i in range(seqs_per_step):
        seq = step * seqs_per_step + i
        qb_sc[...] = jnp.broadcast_to(_column(qt_ref, seq), qb_sc.shape)
        for c in range(cps):
            for p in range(chunk_pages):
                k_copy(seq, c, p).wait()

            for p in range(chunk_pages):
                sc_ref[i, c * chunk_pages + p] = _page_scores(kbuf[c, p], qb_sc[...])
            host_work(i * cps + c)

            @pl.when(seq + 1 < n_seq)
            def _(seq=seq, c=c):
                start_chunk(seq + 1, c)


def _sample_attn_kernel(pt_ref, sc_ref, q_ref, kn_ref, vn_ref, cv_hbm, o_ref,
                        vbuf, s_sc, sem_v, *, n_pages, page):
    step = pl.program_id(0)
    group = sc_ref.shape[0]

    @pl.when(step == 0)
    def _():
        o_ref[...] = jnp.zeros_like(o_ref)

    selections = [_sample_select(step * group + g, g, pt_ref, sc_ref, q_ref, kn_ref, cv_hbm, vbuf, sem_v,
                                 n_pages=n_pages, page=page) for g in range(group)]
    for g, sel in enumerate(selections):
        _sample_finish(step * group + g, g, sel, sc_ref, vn_ref, o_ref, vbuf, s_sc, n_pages=n_pages, page=page)


def _sample_select(b, g, pt_ref, sc_ref, q_ref, kn_ref, cv_hbm, vbuf, sem_v, *, n_pages, page):
    ppb = MOBA_BLOCK // page
    n_past = n_pages // ppb
    n_vbuf = vbuf.shape[0] // sc_ref.shape[0]
    q_col, kn_col = _column(q_ref, b), _column(kn_ref, b)

    pos_w = lax.broadcasted_iota(jnp.int32, (ATTN_W, page), 1)
    s_self = _page_scores(jnp.where(pos_w == 0, kn_col, 0.0), q_col)

    hl = lax.broadcasted_iota(jnp.int32, (N_HEADS, LANES), 1)
    cand = jnp.where(hl == n_past, NEG * SCALE, -jnp.inf).astype(F32)
    for j in range(n_past):
        blk = sc_ref[g, j * ppb]
        for r in range(1, ppb):
            blk = blk + sc_ref[g, j * ppb + r]
        cand = jnp.where(hl == j, jnp.sum(blk, axis=1, keepdims=True) * (1.0 / MOBA_BLOCK), cand)
    rank = jnp.zeros((N_HEADS, LANES), F32)
    for j in range(n_past + 1):
        gj = cand[:, j:j + 1]
        rank = rank + jnp.where(gj > cand, 1.0, jnp.where(gj == cand, jnp.where(hl > j, 1.0, 0.0), 0.0))
    hl_f = hl.astype(F32)
    picks = [jnp.sum(jnp.where(rank == t, hl_f, 0.0), axis=1, keepdims=True).astype(jnp.int32)
             for t in range(MOBA_TOPK)]

    blocks = [[picks[t][h, 0] for t in range(MOBA_TOPK)] for h in range(N_HEADS)]
    clamped = [[jnp.minimum(blk, n_past - 1) for blk in row] for row in blocks]

    def v_copy(h, t, r):
        phys = pt_ref[b, clamped[h][t] * ppb + r]
        return pltpu.make_async_copy(cv_hbm.at[phys, pl.ds(h * HEAD_DIM, HEAD_DIM), :],
                                     vbuf.at[g * n_vbuf + (h * MOBA_TOPK + t) * ppb + r], sem_v.at[g])

    copies = [v_copy(h, t, r) for h in range(N_HEADS) for t in range(MOBA_TOPK) for r in range(ppb)]
    for n, copy in enumerate(copies):
        copy.start(priority=n % DMA_PRIORITIES)
    return dict(s_self=s_self, picks=picks, blocks=blocks, clamped=clamped, copies=copies)


def _sample_finish(b, g, sel, sc_ref, vn_ref, o_ref, vbuf, s_sc, *, n_pages, page):
    ppb = MOBA_BLOCK // page
    n_past = n_pages // ppb
    n_vbuf = vbuf.shape[0] // sc_ref.shape[0]
    s_self, picks, blocks, clamped = sel["s_self"], sel["picks"], sel["blocks"], sel["clamped"]
    hl = lax.broadcasted_iota(jnp.int32, (N_HEADS, LANES), 1)
    vn_col = _column(vn_ref, b)

    s_all = sc_ref[g]
    blk_id = lax.broadcasted_iota(jnp.int32, s_all.shape, 0) // ppb
    chosen = blk_id == picks[0][None]
    for t in range(1, MOBA_TOPK):
        chosen = chosen | (blk_id == picks[t][None])
    s_all = jnp.where(chosen, s_all, NEG)
    s_self = jnp.where(hl == 0, s_self, NEG)
    m = jnp.max(jnp.maximum(jnp.max(s_all, axis=0), s_self), axis=1, keepdims=True)
    p_all = jnp.exp(s_all - m[None])
    p_self = jnp.exp(s_self - m)
    denom = jnp.sum(jnp.sum(p_all, axis=0) + p_self, axis=1, keepdims=True)
    s_sc[g] = p_all / denom[None]
    p_self = p_self / denom

    for copy in sel["copies"]:
        copy.wait()

    pos_h = lax.broadcasted_iota(jnp.int32, (HEAD_DIM, page), 1)
    cols = []
    for h in range(N_HEADS):
        rows = slice(h * HEAD_DIM, (h + 1) * HEAD_DIM)
        acc = p_self[h:h + 1, :] * jnp.where(pos_h == 0, vn_col[rows], 0.0)
        for t in range(MOBA_TOPK):
            for r in range(ppb):
                p_row = s_sc[g, clamped[h][t] * ppb + r, pl.ds(h, 1), :]
                p_row = jnp.where(blocks[h][t] < n_past, p_row, 0.0)
                acc = acc + p_row * vbuf[g * n_vbuf + (h * MOBA_TOPK + t) * ppb + r]
        cols.append(jnp.sum(acc, axis=1, keepdims=True))
    o_col = jnp.concatenate(cols, axis=0)
    seq_id = lax.broadcasted_iota(jnp.int32, o_ref.shape, 1)
    o_ref[...] = jnp.where(seq_id == b, o_col, o_ref[...])


def _channel_major_pages(cache):
    n_phys, page = cache.shape[0], cache.shape[1]
    return cache.transpose(0, 2, 3, 1).reshape(n_phys, ATTN_W, page)


def _sample_attention(scores, q_t, k_new_t, v_new_t, cache_v, page_table):
    s, n_pages = page_table.shape
    page = cache_v.shape[1]
    assert page == LANES and MOBA_BLOCK % page == 0 and (n_pages * page) % MOBA_BLOCK == 0
    n_past = n_pages * page // MOBA_BLOCK
    assert 1 <= n_past < LANES and s <= LANES
    n_vbuf = N_HEADS * MOBA_TOPK * (MOBA_BLOCK // page)
    group = SAMPLE_SEQS_PER_STEP if s % SAMPLE_SEQS_PER_STEP == 0 else 1
    tok = pl.BlockSpec((ATTN_W, s), lambda b, pt: (0, 0))
    grid_spec = pltpu.PrefetchScalarGridSpec(
        num_scalar_prefetch=1,
        grid=(s // group,),
        in_specs=[pl.BlockSpec((group, n_pages, N_HEADS, page), lambda b, pt: (b, 0, 0, 0)), tok, tok, tok,
                  pl.BlockSpec(memory_space=pl.ANY)],
        out_specs=tok,
        scratch_shapes=[
            pltpu.VMEM((group * n_vbuf, HEAD_DIM, page), F32),
            pltpu.VMEM((group, n_pages, N_HEADS, page), F32),
            pltpu.SemaphoreType.DMA((group,)),
        ],
    )
    return pl.pallas_call(
        functools.partial(_sample_attn_kernel, n_pages=n_pages, page=page),
        grid_spec=grid_spec,
        out_shape=jax.ShapeDtypeStruct((ATTN_W, s), F32),
        compiler_params=pltpu.CompilerParams(dimension_semantics=("arbitrary",), vmem_limit_bytes=VMEM_LIMIT),
        name="sample_attn",
    )(page_table, scores, q_t, k_new_t, v_new_t, _channel_major_pages(cache_v))


def _merge_kernel(x_ref, oa_ref, u_ref, vv_ref, sga_ref, sgb_ref, ws_ref, bs_ref, wpa_ref, wpb_ref, wo_ref,
                  x1_ref, *, single_position):
    tm = x_ref.shape[0]
    if single_position:
        s = vv_ref[...].astype(F32) * ws_ref[...] + bs_ref[...]
    else:
        t_idx = lax.broadcasted_iota(jnp.int32, (SGU_CHUNK, SGU_CHUNK), 0)
        s_idx = lax.broadcasted_iota(jnp.int32, (SGU_CHUNK, SGU_CHUNK), 1)
        ws = [jnp.where(s_idx <= t_idx, ws_ref[g], 0.0).astype(BF16) for g in range(SGU_GROUPS)]
        lane = lax.broadcasted_iota(jnp.int32, (SGU_CHUNK, LANES), 1)
        groups_per_block = LANES // SGU_GROUP_W
        chunks = []
        for c in range(tm // SGU_CHUNK):
            parts = []
            for blk in range(SGU_W // LANES):
                vp = vv_ref[c * SGU_CHUNK:(c + 1) * SGU_CHUNK, blk * LANES:(blk + 1) * LANES]
                sp = jnp.zeros((SGU_CHUNK, LANES), F32)
                for g in range(groups_per_block):
                    sg = jnp.dot(ws[blk * groups_per_block + g], vp, preferred_element_type=F32)
                    sp = jnp.where(lane // SGU_GROUP_W == g, sg, sp)
                parts.append(sp)
            chunks.append(jnp.concatenate(parts, axis=1) + bs_ref[...])
        s = jnp.concatenate(chunks, axis=0)
    ob = (u_ref[...].astype(F32) * s).astype(BF16)
    ba = jnp.dot(oa_ref[...], wpa_ref[...], preferred_element_type=F32)
    bb = jnp.dot(ob, wpb_ref[...], preferred_element_type=F32)
    merged = (sga_ref[...].astype(F32) * ba + sgb_ref[...].astype(F32) * bb).astype(BF16)
    x1_ref[...] = x_ref[...] + jnp.dot(merged, wo_ref[...], preferred_element_type=F32)


def _merge(x2d, o_a, u, vv, sga, sgb, w_s, b_s, w_pa_bf, w_pb_bf, w_o_bf, *, tm, single_position):
    m, d = x2d.shape
    if single_position:
        ws_in = jnp.repeat(w_s[:, 0, 0], SGU_GROUP_W).reshape(1, SGU_W)
        bs_in = jnp.repeat(b_s[:, 0], SGU_GROUP_W).reshape(1, SGU_W)
    else:
        ws_in = w_s
        bs_in = jnp.repeat(b_s.T, SGU_GROUP_W, axis=1)
    row = lambda w: pl.BlockSpec((tm, w), lambda r: (r, 0))
    return pl.pallas_call(
        functools.partial(_merge_kernel, single_position=single_position),
        grid=(m // tm,),
        in_specs=[row(d), row(ATTN_W), row(SGU_W), row(SGU_W), row(d), row(d),
                  _const_spec(ws_in.shape), _const_spec(bs_in.shape),
                  _const_spec(w_pa_bf.shape), _const_spec(w_pb_bf.shape), _const_spec(w_o_bf.shape)],
        out_specs=row(d),
        out_shape=jax.ShapeDtypeStruct((m, d), F32),
        compiler_params=pltpu.CompilerParams(dimension_semantics=("arbitrary",), vmem_limit_bytes=VMEM_LIMIT),
        name="merge",
    )(x2d, o_a, u, vv, sga, sgb, ws_in, bs_in, w_pa_bf, w_pb_bf, w_o_bf)


FFN_CHUNK = 1024


def _ffn_kernel(x_ref, g_ref, wup_ref, wdn_ref, gf_ref, y_ref):
    x = x_ref[...]
    h = _rmsnorm(x, g_ref[...]).astype(BF16)
    acc = x
    for c in range(wup_ref.shape[1] // FFN_CHUNK):
        a = jnp.dot(h, wup_ref[:, c * FFN_CHUNK:(c + 1) * FFN_CHUNK], preferred_element_type=F32)
        a = jnp.square(jnp.maximum(a, 0.0)).astype(BF16)
        acc = acc + jnp.dot(a, wdn_ref[c * FFN_CHUNK:(c + 1) * FFN_CHUNK, :], preferred_element_type=F32)
    y_ref[...] = _rmsnorm(acc, gf_ref[...])


def _ffn(x2d, g_ffn, w_up_bf, w_down_bf, g_final, *, tm):
    m, d = x2d.shape
    row = pl.BlockSpec((tm, d), lambda r: (r, 0))
    return pl.pallas_call(
        _ffn_kernel,
        grid=(m // tm,),
        in_specs=[row, _const_spec((1, d)), _const_spec(w_up_bf.shape), _const_spec(w_down_bf.shape),
                  _const_spec((1, d))],
        out_specs=row,
        out_shape=jax.ShapeDtypeStruct((m, d), F32),
        compiler_params=pltpu.CompilerParams(dimension_semantics=("arbitrary",), vmem_limit_bytes=VMEM_LIMIT),
        name="ffn",
    )(x2d, g_ffn.reshape(1, d), w_up_bf, w_down_bf, g_final.reshape(1, d))


def _ffn_score_kernel(pt_ref, x_ref, g_ref, wup_ref, wdn_ref, gf_ref, qt_ref, ck_hbm, y_ref, sc_ref,
                      kbuf, qb_sc, h_sc, sem_k):
    n_slices = sc_ref.shape[0] * kbuf.shape[0]
    width = wup_ref.shape[1] // n_slices
    x = x_ref[...]
    h_sc[...] = _rmsnorm(x, g_ref[...]).astype(BF16)
    y_ref[...] = x

    def mlp_slice(k):
        a = jnp.dot(h_sc[...], wup_ref[:, k * width:(k + 1) * width], preferred_element_type=F32)
        a = jnp.square(jnp.maximum(a, 0.0)).astype(BF16)
        y_ref[...] += jnp.dot(a, wdn_ref[k * width:(k + 1) * width, :], preferred_element_type=F32)

    _score_cached_keys(pl.program_id(0), pt_ref, qt_ref, ck_hbm, sc_ref, kbuf, qb_sc, sem_k, mlp_slice)
    y_ref[...] = _rmsnorm(y_ref[...], gf_ref[...])


def _ffn_and_sample_scores(x2d, g_ffn, w_up_bf, w_down_bf, g_final, q_t, cache_k, page_table, *, tm):
    m, d = x2d.shape
    s, n_pages = page_table.shape
    page = cache_k.shape[1]
    n_steps = m // tm
    assert s % n_steps == 0 and s <= LANES
    chunk_pages = min(SAMPLE_CHUNK_PAGES, n_pages)
    assert n_pages % chunk_pages == 0
    assert w_up_bf.shape[1] % ((s // n_steps) * (n_pages // chunk_pages) * LANES) == 0
    row = pl.BlockSpec((tm, d), lambda r, pt: (r, 0))
    const = lambda shape: pl.BlockSpec(shape, lambda r, pt: (0,) * len(shape), pipeline_mode=pl.Buffered(1))
    grid_spec = pltpu.PrefetchScalarGridSpec(
        num_scalar_prefetch=1,
        grid=(n_steps,),
        in_specs=[row, const((1, d)), const(w_up_bf.shape), const(w_down_bf.shape), const((1, d)),
                  const((ATTN_W, s)), pl.BlockSpec(memory_space=pl.ANY)],
        out_specs=[row, pl.BlockSpec((s // n_steps, n_pages, N_HEADS, page), lambda r, pt: (r, 0, 0, 0))],
        scratch_shapes=[
            pltpu.VMEM((n_pages // chunk_pages, chunk_pages, ATTN_W, page), F32),
            pltpu.VMEM((ATTN_W, page), F32),
            pltpu.VMEM((tm, d), BF16),
            pltpu.SemaphoreType.DMA((n_pages // chunk_pages,)),
        ],
    )
    return pl.pallas_call(
        _ffn_score_kernel,
        grid_spec=grid_spec,
        out_shape=[jax.ShapeDtypeStruct((m, d), F32), jax.ShapeDtypeStruct((s, n_pages, N_HEADS, page), F32)],
        compiler_params=pltpu.CompilerParams(dimension_semantics=("arbitrary",), vmem_limit_bytes=VMEM_LIMIT),
        name="ffn_scores",
    )(page_table, x2d, g_ffn.reshape(1, d), w_up_bf, w_down_bf, g_final.reshape(1, d), q_t,
      _channel_major_pages(cache_k))


def _rope_tables(pos):
    half = HEAD_DIM // 2
    inv = ROPE_THETA ** (-2.0 * jnp.arange(half, dtype=F32) / HEAD_DIM)
    ang = pos.astype(F32)[:, None] * inv[None, :]
    cos = jnp.tile(jnp.cos(ang), (1, 2 * N_HEADS))
    sin = jnp.tile(jnp.concatenate([-jnp.sin(ang), jnp.sin(ang)], axis=1), (1, N_HEADS))
    return cos, sin


def kernel(x_prompt, x_sample, cache_k, cache_v, page_table, g_attn, w_in, w_pa, w_pb, w_o, g_v, w_s, b_s,
           g_ffn, w_up, w_down, g_final):
    depth = w_in.shape[0]
    assert depth == 1, "the final norm is fused into the last layer's MLP kernel"
    bsz, seq, d = x_prompt.shape
    n_dec, dec_seq, _ = x_sample.shape
    assert dec_seq == 1 and seq % MOBA_BLOCK == 0
    past_len = page_table.shape[1] * cache_k.shape[2]
    n_chunk_rows = seq - ((seq - 1) // SGU_CHUNK) * SGU_CHUNK
    assert n_chunk_rows == SGU_CHUNK

    cos_p, sin_p = _rope_tables(jnp.arange(seq, dtype=jnp.int32))
    cos_s, sin_s = _rope_tables(jnp.full((n_dec,), past_len, dtype=jnp.int32))

    xp = x_prompt.reshape(bsz * seq, d)
    xs = x_sample.reshape(n_dec, d)
    l = 0
    w_in_bf, w_pa_bf, w_pb_bf, w_o_bf = (w[l].astype(BF16) for w in (w_in, w_pa, w_pb, w_o))
    w_up_bf, w_down_bf = w_up[l].astype(BF16), w_down[l].astype(BF16)

    tm_p = PROMPT_ROW_TILE
    assert seq % tm_p == 0
    (q_t, k_p, v_p, kb, vb_t, u, vv, sga, sgb, sguv_p, kmean) = _inproj(
        xp, g_attn[l], w_in_bf, cos_p, sin_p, g_v[l], tm=MOBA_BLOCK, seq_tiles=seq // MOBA_BLOCK, q_dtype=BF16,
        q_scale=PROMPT_Q_SCALE, emit_kmean=True)
    o_a = _prompt_attention(q_t, kb.reshape(bsz, seq, ATTN_W), vb_t,
                            kmean.reshape(bsz, seq // MOBA_BLOCK, ATTN_W)).reshape(bsz * seq, ATTN_W)
    x1 = _merge(xp, o_a, u, vv, sga, sgb, w_s[l], b_s[l], w_pa_bf, w_pb_bf, w_o_bf, tm=tm_p, single_position=False)

    (q_s, k_s, v_s, _, _, u_s, vv_s, sga_s, sgb_s, sguv_s) = _inproj(
        xs, g_attn[l], w_in_bf, cos_s, sin_s, g_v[l], tm=n_dec, seq_tiles=1, q_dtype=F32, q_scale=SCALE,
        emit_kmean=False)
    y_p, scores = _ffn_and_sample_scores(x1, g_ffn[l], w_up_bf, w_down_bf, g_final, q_s[0], cache_k[l], page_table,
                                         tm=tm_p)
    o_as = _sample_attention(scores, q_s[0], k_s[0], v_s[0], cache_v[l], page_table).T.astype(BF16)
    x1_s = _merge(xs, o_as, u_s, vv_s, sga_s, sgb_s, w_s[l], b_s[l], w_pa_bf, w_pb_bf, w_o_bf,
                  tm=n_dec, single_position=True)
    y_s = _ffn(x1_s, g_ffn[l], w_up_bf, w_down_bf, g_final, tm=n_dec)

    def heads_last(t, n, length):
        return t.reshape(1, n, N_HEADS, HEAD_DIM, length).transpose(0, 1, 4, 2, 3)

    return (y_p.reshape(bsz, seq, d),
            y_s.reshape(n_dec, 1, d),
            heads_last(k_p, bsz, seq),
            heads_last(v_p, bsz, seq),
            sguv_p.reshape(1, bsz, SGU_CHUNK, SGU_W),
            heads_last(k_s, 1, n_dec).reshape(1, n_dec, 1, N_HEADS, HEAD_DIM),
            heads_last(v_s, 1, n_dec).reshape(1, n_dec, 1, N_HEADS, HEAD_DIM),
            sguv_s.reshape(1, n_dec, 1, SGU_W))
```
